```python
import jax, jax.numpy as jnp
from jax import lax
import numpy as np


D_MODEL = 2048
BATCH = 16
SEQ = 2048
DEPTH = 4

N_A_LAYERS = DEPTH // 2
N_B_LAYERS = DEPTH - N_A_LAYERS
HEAD_DIM = 128
N_MIX_HEADS = D_MODEL // HEAD_DIM
MEM_HEADS = 4
SB_HEADS = N_MIX_HEADS - MEM_HEADS
MLA_HEADS = N_MIX_HEADS - MEM_HEADS
MLA_NOPE_DIM = 128
MLA_ROPE_DIM = 64
MLA_V_DIM = 128
Q_LORA_RANK = 512
KV_LORA_RANK = 512
MEM_LEN = 256
FFN_HIDDEN = -(-8 * D_MODEL // (3 * 256)) * 256
BLOCK_Q = 128
ROPE_THETA = 10000.0
RMS_EPS = 1e-6

A_IN_WIDTH = 3 * SB_HEADS * HEAD_DIM + MEM_HEADS * HEAD_DIM
B_IN_WIDTH = Q_LORA_RANK + MEM_HEADS * HEAD_DIM
A_OUT_WIDTH = SB_HEADS * HEAD_DIM + MEM_HEADS * HEAD_DIM
B_OUT_WIDTH = MLA_HEADS * MLA_V_DIM + MEM_HEADS * HEAD_DIM

kernel_name = "yoco_stickbreak_mla_hybrid"


def rmsnorm(x, g):
    x32 = x.astype(jnp.float32)
    y = x32 * lax.rsqrt(jnp.mean(x32 * x32, axis=-1, keepdims=True) + RMS_EPS)
    return y.astype(x.dtype) * g


def rope_tables(positions, dtype):
    half = MLA_ROPE_DIM // 2
    inv_freq = ROPE_THETA ** (-jnp.arange(half, dtype=jnp.float32) / half)
    ang = positions.astype(jnp.float32)[..., None] * inv_freq
    return jnp.cos(ang).astype(dtype), jnp.sin(ang).astype(dtype)


def apply_rope(x, cos, sin):
    half = x.shape[-1] // 2
    x1, x2 = x[..., :half], x[..., half:]
    return jnp.concatenate([x1 * cos - x2 * sin, x2 * cos + x1 * sin], axis=-1)


def to_query_blocks(t):
    b, s, h, d = t.shape
    return t.reshape(b, s // BLOCK_Q, BLOCK_Q, h, d).transpose(1, 0, 3, 2, 4)


def from_query_blocks(t):
    nb, b, h, bq, d = t.shape
    return t.transpose(1, 0, 3, 2, 4).reshape(b, nb * bq, h * d)


def stick_breaking_attention(q, k, v):
    seq = q.shape[1]
    scale = q.shape[-1] ** -0.5
    kh = k.transpose(0, 2, 1, 3)
    vh = v.transpose(0, 2, 1, 3)
    key_pos = jnp.arange(seq)
    starts = jnp.arange(seq // BLOCK_Q) * BLOCK_Q

    def one_block(args):
        q_blk, start = args
        z = jnp.einsum("bhqd,bhkd->bhqk", q_blk, kh).astype(jnp.float32) * scale
        query_pos = start + jnp.arange(BLOCK_Q)
        strictly_past = key_pos[None, :] < query_pos[:, None]
        log_keep = jnp.where(strictly_past, -jax.nn.softplus(z), 0.0)
        log_stick = lax.cumsum(log_keep, axis=3, reverse=True) - log_keep
        w = jnp.where(strictly_past, jnp.exp(jax.nn.log_sigmoid(z) + log_stick), 0.0)
        return jnp.einsum("bhqk,bhkd->bhqd", w.astype(vh.dtype), vh)

    out = lax.map(one_block, (to_query_blocks(q), starts))
    return from_query_blocks(out)


def mla_attention(q_nope, q_rope, k_nope, k_rope, v):
    seq = q_nope.shape[1]
    scale = (MLA_NOPE_DIM + MLA_ROPE_DIM) ** -0.5
    knh = k_nope.transpose(0, 2, 1, 3)
    vh = v.transpose(0, 2, 1, 3)
    key_pos = jnp.arange(seq)
    starts = jnp.arange(seq // BLOCK_Q) * BLOCK_Q

    def one_block(args):
        qn, qr, start = args
        s = (jnp.einsum("bhqd,bhkd->bhqk", qn, knh)
             + jnp.einsum("bhqr,bkr->bhqk", qr, k_rope)).astype(jnp.float32) * scale
        causal = key_pos[None, :] <= (start + jnp.arange(BLOCK_Q))[:, None]
        p = jax.nn.softmax(jnp.where(causal, s, -jnp.inf), axis=-1)
        return jnp.einsum("bhqk,bhkd->bhqd", p.astype(vh.dtype), vh)

    out = lax.map(one_block, (to_query_blocks(q_nope), to_query_blocks(q_rope), starts))
    return from_query_blocks(out)


def memory_attention(q, k, v):
    b, s, hm, d = q.shape
    sc = jnp.einsum("bshd,bmhd->bhsm", q, k).astype(jnp.float32) * d ** -0.5
    p = jax.nn.softmax(sc, axis=-1)
    o = jnp.einsum("bhsm,bmhd->bshd", p.astype(v.dtype), v)
    return o.reshape(b, s, hm * d)


def swiglu(x, w_gate_up, w_down):
    gate, up = jnp.split(x @ w_gate_up, 2, axis=-1)
    return (jax.nn.silu(gate) * up) @ w_down


def shared_latent_kv(h, kv_norm_g, w_dkv, kv_latent_g, w_ukv, cos, sin):
    b, s, _ = h.shape
    ckv = rmsnorm(h, kv_norm_g) @ w_dkv
    c_latent = rmsnorm(ckv[..., :KV_LORA_RANK], kv_latent_g)
    k_rope = apply_rope(ckv[..., KV_LORA_RANK:], cos, sin)
    kv = (c_latent @ w_ukv).reshape(b, s, MLA_HEADS, MLA_NOPE_DIM + MLA_V_DIM)
    return kv[..., :MLA_NOPE_DIM], k_rope, kv[..., MLA_NOPE_DIM:]


def setup_inputs(seed: int = 0) -> dict:
    key = jax.random.key(seed)
    ks = jax.random.split(key, 24)

    def dense(k, shape):
        return jax.random.normal(k, shape, jnp.float32) * shape[-2] ** -0.5

    def gain(k, shape):
        return 1.0 + 0.02 * jax.random.normal(k, shape, jnp.float32)

    start = jax.random.randint(ks[2], (BATCH, 1), 0, 4096, dtype=jnp.int32)
    positions = start + jnp.arange(SEQ, dtype=jnp.int32)[None, :]
    return {
        "x": jax.random.normal(ks[0], (BATCH, SEQ, D_MODEL), jnp.float32),
        "mem": jax.random.normal(ks[1], (BATCH, MEM_LEN, D_MODEL), jnp.float32),
        "positions": positions,
        "attn_norm_g": gain(ks[3], (DEPTH, D_MODEL)),
        "ffn_norm_g": gain(ks[4], (DEPTH, D_MODEL)),
        "a_w_in": dense(ks[5], (N_A_LAYERS, D_MODEL, A_IN_WIDTH)),
        "a_w_out": dense(ks[6], (N_A_LAYERS, A_OUT_WIDTH, D_MODEL)),
        "b_w_in": dense(ks[7], (N_B_LAYERS, D_MODEL, B_IN_WIDTH)),
        "b_q_norm_g": gain(ks[8], (N_B_LAYERS, Q_LORA_RANK)),
        "b_w_uq": dense(ks[9], (N_B_LAYERS, Q_LORA_RANK, MLA_HEADS * (MLA_NOPE_DIM + MLA_ROPE_DIM))),
        "b_w_out": dense(ks[10], (N_B_LAYERS, B_OUT_WIDTH, D_MODEL)),
        "mem_norm_g": gain(ks[11], (D_MODEL,)),
        "w_mem_kv": dense(ks[12], (DEPTH, D_MODEL, 2 * MEM_HEADS * HEAD_DIM)),
        "kv_norm_g": gain(ks[13], (D_MODEL,)),
        "w_dkv": dense(ks[14], (D_MODEL, KV_LORA_RANK + MLA_ROPE_DIM)),
        "kv_latent_g": gain(ks[15], (KV_LORA_RANK,)),
        "w_ukv": dense(ks[16], (KV_LORA_RANK, MLA_HEADS * (MLA_NOPE_DIM + MLA_V_DIM))),
        "ffn_w_gu": dense(ks[17], (DEPTH, D_MODEL, 2 * FFN_HIDDEN)),
        "ffn_w_down": dense(ks[18], (DEPTH, FFN_HIDDEN, D_MODEL)),
        "final_norm_g": gain(ks[19], (D_MODEL,)),
    }


def reference(x, mem, positions, attn_norm_g, ffn_norm_g, a_w_in, a_w_out, b_w_in,
              b_q_norm_g, b_w_uq, b_w_out, mem_norm_g, w_mem_kv, kv_norm_g, w_dkv,
              kv_latent_g, w_ukv, ffn_w_gu, ffn_w_down, final_norm_g):
    b, s, _ = x.shape
    cos, sin = rope_tables(positions, x.dtype)
    mem_n = rmsnorm(mem, mem_norm_g)
    sb_w = SB_HEADS * HEAD_DIM
    mq_w = MEM_HEADS * HEAD_DIM
    h = x
    for layer in range(DEPTH):
        if layer == N_A_LAYERS:
            k_nope, k_rope, v_lat = shared_latent_kv(h, kv_norm_g, w_dkv, kv_latent_g, w_ukv, cos, sin)
        xn = rmsnorm(h, attn_norm_g[layer])
        mkv = (mem_n @ w_mem_kv[layer]).reshape(b, MEM_LEN, 2, MEM_HEADS, HEAD_DIM)
        if layer < N_A_LAYERS:
            proj = xn @ a_w_in[layer]
            q = proj[..., :sb_w].reshape(b, s, SB_HEADS, HEAD_DIM)
            k = proj[..., sb_w:2 * sb_w].reshape(b, s, SB_HEADS, HEAD_DIM)
            v = proj[..., 2 * sb_w:3 * sb_w].reshape(b, s, SB_HEADS, HEAD_DIM)
            q_mem = proj[..., 3 * sb_w:].reshape(b, s, MEM_HEADS, HEAD_DIM)
            mix = stick_breaking_attention(q, k, v)
            w_out = a_w_out[layer]
        else:
            i = layer - N_A_LAYERS
            proj = xn @ b_w_in[i]
            c_q = rmsnorm(proj[..., :Q_LORA_RANK], b_q_norm_g[i])
            q_mem = proj[..., Q_LORA_RANK:].reshape(b, s, MEM_HEADS, HEAD_DIM)
            q = (c_q @ b_w_uq[i]).reshape(b, s, MLA_HEADS, MLA_NOPE_DIM + MLA_ROPE_DIM)
            q_rope = apply_rope(q[..., MLA_NOPE_DIM:], cos[:, :, None, :], sin[:, :, None, :])
            mix = mla_attention(q[..., :MLA_NOPE_DIM], q_rope, k_nope, k_rope, v_lat)
            w_out = b_w_out[i]
        mem_out = memory_attention(q_mem, mkv[:, :, 0], mkv[:, :, 1])
        h = h + jnp.concatenate([mix, mem_out], axis=-1) @ w_out
        h = h + swiglu(rmsnorm(h, ffn_norm_g[layer]), ffn_w_gu[layer], ffn_w_down[layer])
    return rmsnorm(h, final_norm_g)
```

```python
import functools

import jax
import jax.numpy as jnp
from jax import lax
from jax.experimental import pallas as pl
from jax.experimental.pallas import tpu as pltpu

HEAD_DIM = 128
MEM_HEADS = 4
MLA_NOPE_DIM = 128
MLA_ROPE_DIM = 64
MLA_V_DIM = 128
ROPE_THETA = 10000.0
RMS_EPS = 1e-6

LANES = 128
MLA_QK_PAD = 2 * LANES
VMEM_LIMIT_BYTES = 48 * 1024 * 1024
NORM_CHUNK_ROWS = 256

BF16 = jnp.bfloat16
F32 = jnp.float32


def _params(*semantics):
    return pltpu.CompilerParams(dimension_semantics=semantics, vmem_limit_bytes=VMEM_LIMIT_BYTES)


def _tile(n, pref):
    t = min(n, pref)
    assert n % t == 0, (n, pref)
    return t


def _rms(x, g):
    return (x * lax.rsqrt(jnp.mean(x * x, axis=-1, keepdims=True) + RMS_EPS)) * g


def _rmsnorm_rows(x_ref, g_ref, out_ref):
    rows = x_ref.shape[0]
    chunk = min(rows, NORM_CHUNK_ROWS)
    g = g_ref[...]

    def body(c, carry):
        r = pl.multiple_of(c * chunk, chunk)
        out_ref[pl.ds(r, chunk), :] = _rms(x_ref[pl.ds(r, chunk), :], g).astype(out_ref.dtype)
        return carry

    lax.fori_loop(0, rows // chunk, body, 0)


def _rope(x, ca, cb):
    partner = pltpu.roll(x, 96, 1) + pltpu.roll(x, 32, 1)
    return x * ca + partner * cb


def _norm_matmul_kernel(h_ref, g_ref, w_ref, o_ref, xn_ref):
    @pl.when(pl.program_id(1) == 0)
    def _():
        _rmsnorm_rows(h_ref, g_ref, xn_ref)

    o_ref[...] = jnp.dot(xn_ref[...], w_ref[...], preferred_element_type=F32).astype(o_ref.dtype)


def _norm_matmul(h, g, w, *, tm, tn):
    m, d = h.shape
    n = w.shape[1]
    tm, tn = _tile(m, tm), _tile(n, tn)
    return pl.pallas_call(
        _norm_matmul_kernel,
        out_shape=jax.ShapeDtypeStruct((m, n), BF16),
        grid=(m // tm, n // tn),
        in_specs=[
            pl.BlockSpec((tm, d), lambda i, j: (i, 0)),
            pl.BlockSpec((1, d), lambda i, j: (0, 0)),
            pl.BlockSpec((d, tn), lambda i, j: (0, j)),
        ],
        out_specs=pl.BlockSpec((tm, tn), lambda i, j: (i, j)),
        scratch_shapes=[pltpu.VMEM((tm, d), BF16)],
        compiler_params=_params("parallel", "arbitrary"),
        name="norm_matmul",
    )(h, g.reshape(1, d), w)


def _ffn_kernel(h_ref, g_ref, wg_ref, wu_ref, wd_ref, gf_ref, o_ref, xn_ref, *, final_norm):
    j = pl.program_id(1)

    @pl.when(j == 0)
    def _():
        _rmsnorm_rows(h_ref, g_ref, xn_ref)
        o_ref[...] = h_ref[...]

    xn = xn_ref[...]
    gate = jnp.dot(xn, wg_ref[...], preferred_element_type=F32)
    up = jnp.dot(xn, wu_ref[...], preferred_element_type=F32)
    act = (gate * jax.nn.sigmoid(gate) * up).astype(BF16)
    o_ref[...] += jnp.dot(act, wd_ref[...], preferred_element_type=F32)

    if final_norm:
        @pl.when(j == pl.num_programs(1) - 1)
        def _():
            _rmsnorm_rows(o_ref, gf_ref, o_ref)


def _ffn(h, g, w_gu, w_d, gf, *, final_norm, tm, tf):
    m, d = h.shape
    f = w_d.shape[0]
    tm, tf = _tile(m, tm), _tile(f, tf)
    nf = f // tf
    return pl.pallas_call(
        functools.partial(_ffn_kernel, final_norm=final_norm),
        out_shape=jax.ShapeDtypeStruct((m, d), F32),
        grid=(m // tm, nf),
        in_specs=[
            pl.BlockSpec((tm, d), lambda i, j: (i, 0)),
            pl.BlockSpec((1, d), lambda i, j: (0, 0)),
            pl.BlockSpec((d, tf), lambda i, j: (0, j)),
            pl.BlockSpec((d, tf), lambda i, j: (0, nf + j)),
            pl.BlockSpec((tf, d), lambda i, j: (j, 0)),
            pl.BlockSpec((1, d), lambda i, j: (0, 0)),
        ],
        out_specs=pl.BlockSpec((tm, d), lambda i, j: (i, 0)),
        scratch_shapes=[pltpu.VMEM((tm, d), BF16)],
        compiler_params=_params("parallel", "arbitrary"),
        name="ffn",
    )(h, g.reshape(1, d), w_gu, w_gu, w_d, gf.reshape(1, d))


def _out_proj_kernel(mix_ref, memo_ref, w1_ref, w2_ref, h_ref, o_ref):
    o_ref[...] = (h_ref[...]
                  + jnp.dot(mix_ref[...], w1_ref[...], preferred_element_type=F32)
                  + jnp.dot(memo_ref[...], w2_ref[...], preferred_element_type=F32))


def _out_proj(mix, memo, w_out, h, *, tm):
    m, d = h.shape
    k1, k2 = mix.shape[1], memo.shape[1]
    tm = _tile(m, tm)
    return pl.pallas_call(
        _out_proj_kernel,
        out_shape=jax.ShapeDtypeStruct((m, d), F32),
        grid=(m // tm,),
        in_specs=[
            pl.BlockSpec((tm, k1), lambda i: (i, 0)),
            pl.BlockSpec((tm, k2), lambda i: (i, 0)),
            pl.BlockSpec((k1, d), lambda i: (0, 0)),
            pl.BlockSpec((k2, d), lambda i: (0, 0)),
            pl.BlockSpec((tm, d), lambda i: (i, 0)),
        ],
        out_specs=pl.BlockSpec((tm, d), lambda i: (i, 0)),
        compiler_params=_params("parallel"),
        name="out_proj",
    )(mix, memo, w_out[:k1], w_out[k1:], h)


def _sb_attn_kernel(q_ref, k_ref, v_ref, tri_ref, o_ref, *, blk, scale):
    i = pl.program_id(2)
    q = q_ref[...]
    tri = tri_ref[...]
    row = lax.broadcasted_iota(jnp.int32, (blk, blk), 0)
    col = lax.broadcasted_iota(jnp.int32, (blk, blk), 1)
    strictly_past = col < row

    def block(j, acc, carry, diagonal):
        ks = pl.multiple_of(j * blk, blk)
        k = k_ref[pl.ds(ks, blk), :]
        v = v_ref[pl.ds(ks, blk), :]
        z = lax.dot_general(q, k, (((1,), (1,)), ((), ())), preferred_element_type=F32) * scale
        t = jnp.log1p(jnp.exp(-jnp.abs(z)))
        log_keep = -(jnp.maximum(z, 0.0) + t)
        log_beta = jnp.minimum(z, 0.0) - t
        if diagonal:
            log_keep = jnp.where(strictly_past, log_keep, 0.0)
        hi = log_keep.astype(BF16)
        lo = (log_keep - hi.astype(F32)).astype(BF16)
        within = (jnp.dot(hi, tri, preferred_element_type=F32)
                  + jnp.dot(lo, tri, preferred_element_type=F32))
        w = jnp.exp(log_beta + within + carry)
        if diagonal:
            w = jnp.where(strictly_past, w, 0.0)
        acc = acc + jnp.dot(w.astype(BF16), v, preferred_element_type=F32)
        carry = carry + jnp.sum(log_keep, axis=1, keepdims=True)
        return acc, carry

    acc0 = jnp.zeros((blk, HEAD_DIM), F32)
    carry0 = jnp.zeros((blk, 1), F32)
    acc, carry = block(i, acc0, carry0, True)

    def body(n, c):
        return block(i - 1 - n, c[0], c[1], False)

    acc, carry = lax.fori_loop(0, i, body, (acc, carry))
    o_ref[...] = acc.astype(o_ref.dtype)


def _sb_attention(proj, n_heads, *, blk):
    b, s, _ = proj.shape
    blk = _tile(s, blk)
    tri = (jnp.arange(blk)[:, None] > jnp.arange(blk)[None, :]).astype(BF16)
    return pl.pallas_call(
        functools.partial(_sb_attn_kernel, blk=blk, scale=HEAD_DIM ** -0.5),
        out_shape=jax.ShapeDtypeStruct((b, s, n_heads * HEAD_DIM), BF16),
        grid=(b, n_heads, s // blk),
        in_specs=[
            pl.BlockSpec((None, blk, HEAD_DIM), lambda bi, h, i: (bi, i, h)),
            pl.BlockSpec((None, s, HEAD_DIM), lambda bi, h, i: (bi, 0, n_heads + h)),
            pl.BlockSpec((None, s, HEAD_DIM), lambda bi, h, i: (bi, 0, 2 * n_heads + h)),
            pl.BlockSpec((blk, blk), lambda bi, h, i: (0, 0)),
        ],
        out_specs=pl.BlockSpec((None, blk, HEAD_DIM), lambda bi, h, i: (bi, i, h)),
        compiler_params=_params("parallel", "parallel", "arbitrary"),
        name="sb_attention",
    )(proj, proj, proj, tri)


def _mla_attn_kernel(q_ref, k_ref, v_ref, o_ref, *, blk, scale):
    i = pl.program_id(2)
    q = q_ref[...]
    row = lax.broadcasted_iota(jnp.int32, (blk, blk), 0)
    col = lax.broadcasted_iota(jnp.int32, (blk, blk), 1)
    causal = col <= row

    def scores(j):
        ks = pl.multiple_of(j * blk, blk)
        k = k_ref[pl.ds(ks, blk), :]
        v = v_ref[pl.ds(ks, blk), :]
        s = lax.dot_general(q, k, (((1,), (1,)), ((), ())), preferred_element_type=F32) * scale
        return s, v

    s, v = scores(i)
    s = jnp.where(causal, s, -jnp.inf)
    m = jnp.max(s, axis=1, keepdims=True)
    p = jnp.exp(s - m)
    l = jnp.sum(p, axis=1, keepdims=True)
    acc = jnp.dot(p.astype(BF16), v, preferred_element_type=F32)

    def body(n, c):
        m, l, acc = c
        s, v = scores(i - 1 - n)
        m_new = jnp.maximum(m, jnp.max(s, axis=1, keepdims=True))
        alpha = jnp.exp(m - m_new)
        p = jnp.exp(s - m_new)
        l = alpha * l + jnp.sum(p, axis=1, keepdims=True)
        acc = alpha * acc + jnp.dot(p.astype(BF16), v, preferred_element_type=F32)
        return m_new, l, acc

    m, l, acc = lax.fori_loop(0, i, body, (m, l, acc))
    o_ref[...] = (acc / l).astype(o_ref.dtype)


def _mla_attention(q_cat, k_cat, v, n_heads, *, blk):
    b, s, _ = q_cat.shape
    blk = _tile(s, blk)
    scale = (MLA_NOPE_DIM + MLA_ROPE_DIM) ** -0.5
    return pl.pallas_call(
        functools.partial(_mla_attn_kernel, blk=blk, scale=scale),
        out_shape=jax.ShapeDtypeStruct((b, s, n_heads * MLA_V_DIM), BF16),
        grid=(b, n_heads, s // blk),
        in_specs=[
            pl.BlockSpec((None, blk, MLA_QK_PAD), lambda bi, h, i: (bi, i, h)),
            pl.BlockSpec((None, s, MLA_QK_PAD), lambda bi, h, i: (bi, 0, h)),
            pl.BlockSpec((None, s, MLA_V_DIM), lambda bi, h, i: (bi, 0, h)),
        ],
        out_specs=pl.BlockSpec((None, blk, MLA_V_DIM), lambda bi, h, i: (bi, i, h)),
        compiler_params=_params("parallel", "parallel", "arbitrary"),
        name="mla_attention",
    )(q_cat, k_cat, v)


def _mem_attn_kernel(q_ref, k_ref, v_ref, o_ref, *, scale):
    s = lax.dot_general(q_ref[...], k_ref[...], (((1,), (1,)), ((), ())),
                        preferred_element_type=F32) * scale
    m = jnp.max(s, axis=1, keepdims=True)
    p = jnp.exp(s - m)
    l = jnp.sum(p, axis=1, keepdims=True)
    o = jnp.dot(p.astype(BF16), v_ref[...], preferred_element_type=F32)
    o_ref[...] = (o / l).astype(o_ref.dtype)


def _mem_attention(q_src, q_col0, mkv, kv_col0, *, tq):
    b, s, _ = q_src.shape
    ml = mkv.shape[1]
    tq = _tile(s, tq)
    return pl.pallas_call(
        functools.partial(_mem_attn_kernel, scale=HEAD_DIM ** -0.5),
        out_shape=jax.ShapeDtypeStruct((b, s, MEM_HEADS * HEAD_DIM), BF16),
        grid=(b, MEM_HEADS, s // tq),
        in_specs=[
            pl.BlockSpec((None, tq, HEAD_DIM), lambda bi, h, i: (bi, i, q_col0 + h)),
            pl.BlockSpec((None, ml, HEAD_DIM), lambda bi, h, i: (bi, 0, kv_col0 + h)),
            pl.BlockSpec((None, ml, HEAD_DIM), lambda bi, h, i: (bi, 0, kv_col0 + MEM_HEADS + h)),
        ],
        out_specs=pl.BlockSpec((None, tq, HEAD_DIM), lambda bi, h, i: (bi, i, h)),
        compiler_params=_params("parallel", "parallel", "parallel"),
        name="mem_attention",
    )(q_src, mkv, mkv)


def _latent_kv_kernel(h_ref, g_ref, wdkv_ref, gl_ref, wukv_ref, ca_ref, cb_ref,
                      kcat_ref, v_ref, xn_ref, *, n_heads, rank):
    _rmsnorm_rows(h_ref, g_ref, xn_ref)
    ckv = jnp.dot(xn_ref[...], wdkv_ref[...], preferred_element_type=F32)
    c_latent = _rms(ckv[:, :rank], gl_ref[...]).astype(BF16)
    k_rope = _rope(ckv[:, rank:], ca_ref[...], cb_ref[...]).astype(BF16)
    for h in range(n_heads):
        kv = jnp.dot(c_latent, wukv_ref[:, h * 256:(h + 1) * 256], preferred_element_type=F32)
        kcat_ref[:, h * MLA_QK_PAD:h * MLA_QK_PAD + MLA_NOPE_DIM] = kv[:, :MLA_NOPE_DIM].astype(BF16)
        kcat_ref[:, h * MLA_QK_PAD + MLA_NOPE_DIM:(h + 1) * MLA_QK_PAD] = k_rope
        v_ref[:, h * MLA_V_DIM:(h + 1) * MLA_V_DIM] = kv[:, MLA_NOPE_DIM:].astype(BF16)


def _latent_kv(h, g, w_dkv_pad, g_latent, w_ukv, ca, cb, n_heads, *, tm):
    m, d = h.shape
    rank = g_latent.shape[0]
    tm = _tile(m, tm)
    row = lambda i: (i, 0)
    fixed = lambda i: (0, 0)
    return pl.pallas_call(
        functools.partial(_latent_kv_kernel, n_heads=n_heads, rank=rank),
        out_shape=(jax.ShapeDtypeStruct((m, n_heads * MLA_QK_PAD), BF16),
                   jax.ShapeDtypeStruct((m, n_heads * MLA_V_DIM), BF16)),
        grid=(m // tm,),
        in_specs=[
            pl.BlockSpec((tm, d), row),
            pl.BlockSpec((1, d), fixed),
            pl.BlockSpec(w_dkv_pad.shape, fixed),
            pl.BlockSpec((1, rank), fixed),
            pl.BlockSpec(w_ukv.shape, fixed),
            pl.BlockSpec((tm, LANES), row),
            pl.BlockSpec((tm, LANES), row),
        ],
        out_specs=(pl.BlockSpec((tm, n_heads * MLA_QK_PAD), row),
                   pl.BlockSpec((tm, n_heads * MLA_V_DIM), row)),
        scratch_shapes=[pltpu.VMEM((tm, d), BF16)],
        compiler_params=_params("parallel"),
        name="latent_kv",
    )(h, g.reshape(1, d), w_dkv_pad, g_latent.reshape(1, rank), w_ukv, ca, cb)


def _mla_in_kernel(h_ref, g_ref, win_ref, gq_ref, wuq_ref, ca_ref, cb_ref,
                   qcat_ref, memq_ref, xn_ref, *, n_heads, rank):
    _rmsnorm_rows(h_ref, g_ref, xn_ref)
    proj = jnp.dot(xn_ref[...], win_ref[...], preferred_element_type=F32)
    memq_ref[...] = proj[:, rank:].astype(BF16)
    c_q = _rms(proj[:, :rank], gq_ref[...]).astype(BF16)
    ca, cb = ca_ref[...], cb_ref[...]
    for h in range(n_heads):
        q = jnp.dot(c_q, wuq_ref[:, h * MLA_QK_PAD:(h + 1) * MLA_QK_PAD], preferred_element_type=F32)
        qcat_ref[:, h * MLA_QK_PAD:h * MLA_QK_PAD + MLA_NOPE_DIM] = q[:, :MLA_NOPE_DIM].astype(BF16)
        qcat_ref[:, h * MLA_QK_PAD + MLA_NOPE_DIM:(h + 1) * MLA_QK_PAD] = (
            _rope(q[:, MLA_NOPE_DIM:], ca, cb).astype(BF16))


def _mla_in(h, g, w_in, g_q, w_uq_pad, ca, cb, n_heads, *, tm):
    m, d = h.shape
    rank = g_q.shape[0]
    n_memq = w_in.shape[1] - rank
    tm = _tile(m, tm)
    row = lambda i: (i, 0)
    fixed = lambda i: (0, 0)
    return pl.pallas_call(
        functools.partial(_mla_in_kernel, n_heads=n_heads, rank=rank),
        out_shape=(jax.ShapeDtypeStruct((m, n_heads * MLA_QK_PAD), BF16),
                   jax.ShapeDtypeStruct((m, n_memq), BF16)),
        grid=(m // tm,),
        in_specs=[
            pl.BlockSpec((tm, d), row),
            pl.BlockSpec((1, d), fixed),
            pl.BlockSpec(w_in.shape, fixed),
            pl.BlockSpec((1, rank), fixed),
            pl.BlockSpec(w_uq_pad.shape, fixed),
            pl.BlockSpec((tm, LANES), row),
            pl.BlockSpec((tm, LANES), row),
        ],
        out_specs=(pl.BlockSpec((tm, n_heads * MLA_QK_PAD), row),
                   pl.BlockSpec((tm, n_memq), row)),
        scratch_shapes=[pltpu.VMEM((tm, d), BF16)],
        compiler_params=_params("parallel"),
        name="mla_in",
    )(h, g.reshape(1, d), w_in, g_q.reshape(1, rank), w_uq_pad, ca, cb)


def _rope_tables(positions):
    half = MLA_ROPE_DIM // 2
    inv_freq = ROPE_THETA ** (-jnp.arange(half, dtype=F32) / half)
    ang = positions.astype(F32)[..., None] * inv_freq
    cos, sin = jnp.cos(ang), jnp.sin(ang)
    zeros = jnp.zeros(cos.shape[:-1] + (LANES - MLA_ROPE_DIM,), F32)
    ca = jnp.concatenate([cos, cos, zeros], axis=-1)
    cb = jnp.concatenate([-sin, sin, zeros], axis=-1)
    return ca.reshape(-1, LANES), cb.reshape(-1, LANES)


def kernel(x, mem, positions, attn_norm_g, ffn_norm_g, a_w_in, a_w_out, b_w_in, b_q_norm_g, b_w_uq, b_w_out, mem_norm_g, w_mem_kv, kv_norm_g, w_dkv, kv_latent_g, w_ukv, ffn_w_gu, ffn_w_down, final_norm_g):
    b, s, d = x.shape
    mem_len = mem.shape[1]
    depth = attn_norm_g.shape[0]
    n_a = a_w_in.shape[0]
    n_b = b_w_in.shape[0]
    mq_w = MEM_HEADS * HEAD_DIM
    sb_heads = (a_w_in.shape[2] - mq_w) // (3 * HEAD_DIM)
    q_rank = b_q_norm_g.shape[1]
    kv_rank = kv_latent_g.shape[0]
    mla_heads = b_w_uq.shape[2] // (MLA_NOPE_DIM + MLA_ROPE_DIM)

    a_w_in_b = a_w_in.astype(BF16)
    a_w_out_b = a_w_out.astype(BF16)
    b_w_in_b = b_w_in.astype(BF16)
    b_w_out_b = b_w_out.astype(BF16)
    w_gu_b = ffn_w_gu.astype(BF16)
    w_d_b = ffn_w_down.astype(BF16)
    w_ukv_b = w_ukv.astype(BF16)
    w_mem_all = jnp.transpose(w_mem_kv, (1, 0, 2)).reshape(d, depth * 2 * mq_w).astype(BF16)
    w_dkv_pad = jnp.pad(w_dkv, ((0, 0), (0, LANES - MLA_ROPE_DIM))).astype(BF16)
    w_uq_pad = jnp.pad(
        b_w_uq.reshape(n_b, q_rank, mla_heads, MLA_NOPE_DIM + MLA_ROPE_DIM),
        ((0, 0), (0, 0), (0, 0), (0, MLA_QK_PAD - MLA_NOPE_DIM - MLA_ROPE_DIM)),
    ).reshape(n_b, q_rank, mla_heads * MLA_QK_PAD).astype(BF16)
    ca, cb = _rope_tables(positions)

    h = x.reshape(b * s, d)
    mkv = _norm_matmul(mem.reshape(b * mem_len, d), mem_norm_g, w_mem_all, tm=1024, tn=1024)
    mkv = mkv.reshape(b, mem_len, depth * 2 * mq_w)
    kv_blocks = 2 * MEM_HEADS

    k_cat = v_lat = None
    for layer in range(depth):
        if layer == n_a:
            k_cat, v_lat = _latent_kv(h, kv_norm_g, w_dkv_pad, kv_latent_g, w_ukv_b, ca, cb,
                                      mla_heads, tm=512)
            k_cat = k_cat.reshape(b, s, -1)
            v_lat = v_lat.reshape(b, s, -1)
        if layer < n_a:
            proj = _norm_matmul(h, attn_norm_g[layer], a_w_in_b[layer], tm=1024, tn=1024)
            proj = proj.reshape(b, s, -1)
            mix = _sb_attention(proj, sb_heads, blk=256)
            memo = _mem_attention(proj, 3 * sb_heads, mkv, layer * kv_blocks, tq=2048)
            w_out = a_w_out_b[layer]
        else:
            li = layer - n_a
            q_cat, mem_q = _mla_in(h, attn_norm_g[layer], b_w_in_b[li], b_q_norm_g[li], w_uq_pad[li],
                                   ca, cb, mla_heads, tm=512)
            mix = _mla_attention(q_cat.reshape(b, s, -1), k_cat, v_lat, mla_heads, blk=256)
            memo = _mem_attention(mem_q.reshape(b, s, -1), 0, mkv, layer * kv_blocks, tq=2048)
            w_out = b_w_out_b[li]
        h = _out_proj(mix.reshape(b * s, -1), memo.reshape(b * s, -1), w_out, h, tm=512)
        h = _ffn(h, ffn_norm_g[layer], w_gu_b[layer], w_d_b[layer], final_norm_g,
                 final_norm=(layer == depth - 1), tm=512, tf=512)
    return h.reshape(b, s, d)
```

```python
import functools

import jax
import jax.numpy as jnp
from jax import lax
from jax.experimental import pallas as pl
from jax.experimental.pallas import tpu as pltpu

HEAD_DIM = 128
MEM_HEADS = 4
MLA_NOPE_DIM = 128
MLA_ROPE_DIM = 64
MLA_V_DIM = 128
ROPE_THETA = 10000.0
RMS_EPS = 1e-6

LANES = 128
MLA_QK_PAD = 2 * LANES
VMEM_LIMIT_BYTES = 48 * 1024 * 1024
NORM_CHUNK_ROWS = 256

BF16 = jnp.bfloat16
F32 = jnp.float32


def _params(*semantics):
    return pltpu.CompilerParams(dimension_semantics=semantics, vmem_limit_bytes=VMEM_LIMIT_BYTES)


def _tile(n, pref):
    t = min(n, pref)
    assert n % t == 0, (n, pref)
    return t


def _rms(x, g):
    return (x * lax.rsqrt(jnp.mean(x * x, axis=-1, keepdims=True) + RMS_EPS)) * g


def _rmsnorm_rows(x_ref, g_ref, out_ref):
    rows = x_ref.shape[0]
    chunk = min(rows, NORM_CHUNK_ROWS)
    g = g_ref[...]

    def body(c, carry):
        r = pl.multiple_of(c * chunk, chunk)
        out_ref[pl.ds(r, chunk), :] = _rms(x_ref[pl.ds(r, chunk), :], g).astype(out_ref.dtype)
        return carry

    lax.fori_loop(0, rows // chunk, body, 0)


def _rope(x, ca, cb):
    partner = pltpu.roll(x, 96, 1) + pltpu.roll(x, 32, 1)
    return x * ca + partner * cb


def _norm_matmul_kernel(h_ref, g_ref, w_ref, o_ref, xn_ref):
    @pl.when(pl.program_id(1) == 0)
    def _():
        _rmsnorm_rows(h_ref, g_ref, xn_ref)

    o_ref[...] = jnp.dot(xn_ref[...], w_ref[...], preferred_element_type=F32).astype(o_ref.dtype)


def _norm_matmul(h, g, w, *, tm, tn):
    m, d = h.shape
    n = w.shape[1]
    tm, tn = _tile(m, tm), _tile(n, tn)
    return pl.pallas_call(
        _norm_matmul_kernel,
        out_shape=jax.ShapeDtypeStruct((m, n), BF16),
        grid=(m // tm, n // tn),
        in_specs=[
            pl.BlockSpec((tm, d), lambda i, j: (i, 0)),
            pl.BlockSpec((1, d), lambda i, j: (0, 0)),
            pl.BlockSpec((d, tn), lambda i, j: (0, j)),
        ],
        out_specs=pl.BlockSpec((tm, tn), lambda i, j: (i, j)),
        scratch_shapes=[pltpu.VMEM((tm, d), BF16)],
        compiler_params=_params("parallel", "arbitrary"),
        name="norm_matmul",
    )(h, g.reshape(1, d), w)


def _ffn_kernel(h_ref, g_ref, wg_ref, wu_ref, wd_ref, gf_ref, o_ref, xn_ref, *, final_norm):
    j = pl.program_id(1)

    @pl.when(j == 0)
    def _():
        _rmsnorm_rows(h_ref, g_ref, xn_ref)
        o_ref[...] = h_ref[...]

    xn = xn_ref[...]
    gate = jnp.dot(xn, wg_ref[...], preferred_element_type=F32)
    up = jnp.dot(xn, wu_ref[...], preferred_element_type=F32)
    act = (gate * jax.nn.sigmoid(gate) * up).astype(BF16)
    o_ref[...] += jnp.dot(act, wd_ref[...], preferred_element_type=F32)

    if final_norm:
        @pl.when(j == pl.num_programs(1) - 1)
        def _():
            _rmsnorm_rows(o_ref, gf_ref, o_ref)


def _ffn(h, g, w_gu, w_d, gf, *, final_norm, tm, tf):
    m, d = h.shape
    f = w_d.shape[0]
    tm, tf = _tile(m, tm), _tile(f, tf)
    nf = f // tf
    return pl.pallas_call(
        functools.partial(_ffn_kernel, final_norm=final_norm),
        out_shape=jax.ShapeDtypeStruct((m, d), F32),
        grid=(m // tm, nf),
        in_specs=[
            pl.BlockSpec((tm, d), lambda i, j: (i, 0)),
            pl.BlockSpec((1, d), lambda i, j: (0, 0)),
            pl.BlockSpec((d, tf), lambda i, j: (0, j)),
            pl.BlockSpec((d, tf), lambda i, j: (0, nf + j)),
            pl.BlockSpec((tf, d), lambda i, j: (j, 0)),
            pl.BlockSpec((1, d), lambda i, j: (0, 0)),
        ],
        out_specs=pl.BlockSpec((tm, d), lambda i, j: (i, 0)),
        scratch_shapes=[pltpu.VMEM((tm, d), BF16)],
        compiler_params=_params("parallel", "arbitrary"),
        name="ffn",
    )(h, g.reshape(1, d), w_gu, w_gu, w_d, gf.reshape(1, d))


def _out_proj_kernel(mix_ref, memo_ref, w1_ref, w2_ref, h_ref, o_ref):
    o_ref[...] = (h_ref[...]
                  + jnp.dot(mix_ref[...], w1_ref[...], preferred_element_type=F32)
                  + jnp.dot(memo_ref[...], w2_ref[...], preferred_element_type=F32))


def _out_proj(mix, memo, w_out, h, *, tm):
    m, d = h.shape
    k1, k2 = mix.shape[1], memo.shape[1]
    tm = _tile(m, tm)
    return pl.pallas_call(
        _out_proj_kernel,
        out_shape=jax.ShapeDtypeStruct((m, d), F32),
        grid=(m // tm,),
        in_specs=[
            pl.BlockSpec((tm, k1), lambda i: (i, 0)),
            pl.BlockSpec((tm, k2), lambda i: (i, 0)),
            pl.BlockSpec((k1, d), lambda i: (0, 0)),
            pl.BlockSpec((k2, d), lambda i: (0, 0)),
            pl.BlockSpec((tm, d), lambda i: (i, 0)),
        ],
        out_specs=pl.BlockSpec((tm, d), lambda i: (i, 0)),
        compiler_params=_params("parallel"),
        name="out_proj",
    )(mix, memo, w_out[:k1], w_out[k1:], h)


def _sb_attn_kernel(q_ref, k_ref, v_ref, tri_ref, o_ref, *, blk, scale):
    tri = tri_ref[...]
    row = lax.broadcasted_iota(jnp.int32, (blk, blk), 0)
    col = lax.broadcasted_iota(jnp.int32, (blk, blk), 1)
    strictly_past = col < row
    for i in range(q_ref.shape[0] // blk):
        n = (i + 1) * blk
        q = q_ref[i * blk:n, :]
        z = lax.dot_general(q, k_ref[0:n, :], (((1,), (1,)), ((), ())),
                            preferred_element_type=F32) * scale
        log_keep = -(jnp.maximum(z, 0.0) + jnp.log(1.0 + jnp.exp(-jnp.abs(z))))
        w_blocks = [None] * (i + 1)
        carry = None
        for j in range(i, -1, -1):
            zj = z[:, j * blk:(j + 1) * blk]
            lk = log_keep[:, j * blk:(j + 1) * blk]
            if j == i:
                lk = jnp.where(strictly_past, lk, 0.0)
            hi = lk.astype(BF16)
            lo = (lk - hi.astype(F32)).astype(BF16)
            log_w = zj + (jnp.dot(hi, tri, preferred_element_type=F32)
                          + jnp.dot(lo, tri, preferred_element_type=F32))
            if carry is not None:
                log_w = log_w + carry
            w = jnp.exp(log_w)
            if j == i:
                w = jnp.where(strictly_past, w, 0.0)
            w_blocks[j] = w.astype(BF16)
            block_sum = jnp.sum(lk, axis=1, keepdims=True)
            carry = block_sum if carry is None else carry + block_sum
        w = w_blocks[0] if i == 0 else jnp.concatenate(w_blocks, axis=1)
        o_ref[i * blk:n, :] = jnp.dot(w, v_ref[0:n, :], preferred_element_type=F32).astype(o_ref.dtype)


def _sb_attention(proj, n_heads, *, blk):
    b, s, _ = proj.shape
    blk = _tile(s, blk)
    tri = (jnp.arange(blk)[:, None] >= jnp.arange(blk)[None, :]).astype(BF16)
    return pl.pallas_call(
        functools.partial(_sb_attn_kernel, blk=blk, scale=HEAD_DIM ** -0.5),
        out_shape=jax.ShapeDtypeStruct((b, s, n_heads * HEAD_DIM), BF16),
        grid=(b, n_heads),
        in_specs=[
            pl.BlockSpec((None, s, HEAD_DIM), lambda bi, h: (bi, 0, h)),
            pl.BlockSpec((None, s, HEAD_DIM), lambda bi, h: (bi, 0, n_heads + h)),
            pl.BlockSpec((None, s, HEAD_DIM), lambda bi, h: (bi, 0, 2 * n_heads + h)),
            pl.BlockSpec((blk, blk), lambda bi, h: (0, 0)),
        ],
        out_specs=pl.BlockSpec((None, s, HEAD_DIM), lambda bi, h: (bi, 0, h)),
        compiler_params=_params("parallel", "parallel"),
        name="sb_attention",
    )(proj, proj, proj, tri)


def _mla_attn_kernel(q_ref, k_ref, v_ref, o_ref, *, blk, scale):
    row = lax.broadcasted_iota(jnp.int32, (blk, blk), 0)
    col = lax.broadcasted_iota(jnp.int32, (blk, blk), 1)
    causal = col <= row
    for i in range(q_ref.shape[0] // blk):
        n = (i + 1) * blk
        q = q_ref[i * blk:n, :]
        s = lax.dot_general(q, k_ref[0:n, :], (((1,), (1,)), ((), ())),
                            preferred_element_type=F32) * scale
        diag = jnp.where(causal, s[:, i * blk:n], -jnp.inf)
        s = diag if i == 0 else jnp.concatenate([s[:, :i * blk], diag], axis=1)
        m = jnp.max(s, axis=1, keepdims=True)
        p = jnp.exp(s - m)
        l = jnp.sum(p, axis=1, keepdims=True)
        o = jnp.dot(p.astype(BF16), v_ref[0:n, :], preferred_element_type=F32)
        o_ref[i * blk:n, :] = (o / l).astype(o_ref.dtype)


def _mla_attention(q_cat, k_cat, v, n_heads, *, blk):
    b, s, _ = q_cat.shape
    blk = _tile(s, blk)
    scale = (MLA_NOPE_DIM + MLA_ROPE_DIM) ** -0.5
    return pl.pallas_call(
        functools.partial(_mla_attn_kernel, blk=blk, scale=scale),
        out_shape=jax.ShapeDtypeStruct((b, s, n_heads * MLA_V_DIM), BF16),
        grid=(b, n_heads),
        in_specs=[
            pl.BlockSpec((None, s, MLA_QK_PAD), lambda bi, h: (bi, 0, h)),
            pl.BlockSpec((None, s, MLA_QK_PAD), lambda bi, h: (bi, 0, h)),
            pl.BlockSpec((None, s, MLA_V_DIM), lambda bi, h: (bi, 0, h)),
        ],
        out_specs=pl.BlockSpec((None, s, MLA_V_DIM), lambda bi, h: (bi, 0, h)),
        compiler_params=_params("parallel", "parallel"),
        name="mla_attention",
    )(q_cat, k_cat, v)


def _mem_attn_kernel(q_ref, k_ref, v_ref, o_ref, *, scale):
    s = lax.dot_general(q_ref[...], k_ref[...], (((1,), (1,)), ((), ())),
                        preferred_element_type=F32) * scale
    m = jnp.max(s, axis=1, keepdims=True)
    p = jnp.exp(s - m)
    l = jnp.sum(p, axis=1, keepdims=True)
    o = jnp.dot(p.astype(BF16), v_ref[...], preferred_element_type=F32)
    o_ref[...] = (o / l).astype(o_ref.dtype)


def _mem_attention(q_src, q_col0, mkv, kv_col0, *, tq):
    b, s, _ = q_src.shape
    ml = mkv.shape[1]
    tq = _tile(s, tq)
    return pl.pallas_call(
        functools.partial(_mem_attn_kernel, scale=HEAD_DIM ** -0.5),
        out_shape=jax.ShapeDtypeStruct((b, s, MEM_HEADS * HEAD_DIM), BF16),
        grid=(b, MEM_HEADS, s // tq),
        in_specs=[
            pl.BlockSpec((None, tq, HEAD_DIM), lambda bi, h, i: (bi, i, q_col0 + h)),
            pl.BlockSpec((None, ml, HEAD_DIM), lambda bi, h, i: (bi, 0, kv_col0 + h)),
            pl.BlockSpec((None, ml, HEAD_DIM), lambda bi, h, i: (bi, 0, kv_col0 + MEM_HEADS + h)),
        ],
        out_specs=pl.BlockSpec((None, tq, HEAD_DIM), lambda bi, h, i: (bi, i, h)),
        compiler_params=_params("parallel", "parallel", "parallel"),
        name="mem_attention",
    )(q_src, mkv, mkv)


def _latent_kv_kernel(h_ref, g_ref, wdkv_ref, gl_ref, wukv_ref, ca_ref, cb_ref,
                      kcat_ref, v_ref, xn_ref, *, n_heads, rank):
    _rmsnorm_rows(h_ref, g_ref, xn_ref)
    ckv = jnp.dot(xn_ref[...], wdkv_ref[...], preferred_element_type=F32)
    c_latent = _rms(ckv[:, :rank], gl_ref[...]).astype(BF16)
    k_rope = _rope(ckv[:, rank:], ca_ref[...], cb_ref[...]).astype(BF16)
    for h in range(n_heads):
        kv = jnp.dot(c_latent, wukv_ref[:, h * 256:(h + 1) * 256], preferred_element_type=F32)
        kcat_ref[:, h * MLA_QK_PAD:h * MLA_QK_PAD + MLA_NOPE_DIM] = kv[:, :MLA_NOPE_DIM].astype(BF16)
        kcat_ref[:, h * MLA_QK_PAD + MLA_NOPE_DIM:(h + 1) * MLA_QK_PAD] = k_rope
        v_ref[:, h * MLA_V_DIM:(h + 1) * MLA_V_DIM] = kv[:, MLA_NOPE_DIM:].astype(BF16)


def _latent_kv(h, g, w_dkv_pad, g_latent, w_ukv, ca, cb, n_heads, *, tm):
    m, d = h.shape
    rank = g_latent.shape[0]
    tm = _tile(m, tm)
    row = lambda i: (i, 0)
    fixed = lambda i: (0, 0)
    return pl.pallas_call(
        functools.partial(_latent_kv_kernel, n_heads=n_heads, rank=rank),
        out_shape=(jax.ShapeDtypeStruct((m, n_heads * MLA_QK_PAD), BF16),
                   jax.ShapeDtypeStruct((m, n_heads * MLA_V_DIM), BF16)),
        grid=(m // tm,),
        in_specs=[
            pl.BlockSpec((tm, d), row),
            pl.BlockSpec((1, d), fixed),
            pl.BlockSpec(w_dkv_pad.shape, fixed),
            pl.BlockSpec((1, rank), fixed),
            pl.BlockSpec(w_ukv.shape, fixed),
            pl.BlockSpec((tm, LANES), row),
            pl.BlockSpec((tm, LANES), row),
        ],
        out_specs=(pl.BlockSpec((tm, n_heads * MLA_QK_PAD), row),
                   pl.BlockSpec((tm, n_heads * MLA_V_DIM), row)),
        scratch_shapes=[pltpu.VMEM((tm, d), BF16)],
        compiler_params=_params("parallel"),
        name="latent_kv",
    )(h, g.reshape(1, d), w_dkv_pad, g_latent.reshape(1, rank), w_ukv, ca, cb)


def _mla_in_kernel(h_ref, g_ref, win_ref, gq_ref, wuq_ref, ca_ref, cb_ref,
                   qcat_ref, memq_ref, xn_ref, *, n_heads, rank):
    _rmsnorm_rows(h_ref, g_ref, xn_ref)
    proj = jnp.dot(xn_ref[...], win_ref[...], preferred_element_type=F32)
    memq_ref[...] = proj[:, rank:].astype(BF16)
    c_q = _rms(proj[:, :rank], gq_ref[...]).astype(BF16)
    ca, cb = ca_ref[...], cb_ref[...]
    for h in range(n_heads):
        q = jnp.dot(c_q, wuq_ref[:, h * MLA_QK_PAD:(h + 1) * MLA_QK_PAD], preferred_element_type=F32)
        qcat_ref[:, h * MLA_QK_PAD:h * MLA_QK_PAD + MLA_NOPE_DIM] = q[:, :MLA_NOPE_DIM].astype(BF16)
        qcat_ref[:, h * MLA_QK_PAD + MLA_NOPE_DIM:(h + 1) * MLA_QK_PAD] = (
            _rope(q[:, MLA_NOPE_DIM:], ca, cb).astype(BF16))


def _mla_in(h, g, w_in, g_q, w_uq_pad, ca, cb, n_heads, *, tm):
    m, d = h.shape
    rank = g_q.shape[0]
    n_memq = w_in.shape[1] - rank
    tm = _tile(m, tm)
    row = lambda i: (i, 0)
    fixed = lambda i: (0, 0)
    return pl.pallas_call(
        functools.partial(_mla_in_kernel, n_heads=n_heads, rank=rank),
        out_shape=(jax.ShapeDtypeStruct((m, n_heads * MLA_QK_PAD), BF16),
                   jax.ShapeDtypeStruct((m, n_memq), BF16)),
        grid=(m // tm,),
        in_specs=[
            pl.BlockSpec((tm, d), row),
            pl.BlockSpec((1, d), fixed),
            pl.BlockSpec(w_in.shape, fixed),
            pl.BlockSpec((1, rank), fixed),
            pl.BlockSpec(w_uq_pad.shape, fixed),
            pl.BlockSpec((tm, LANES), row),
            pl.BlockSpec((tm, LANES), row),
        ],
        out_specs=(pl.BlockSpec((tm, n_heads * MLA_QK_PAD), row),
                   pl.BlockSpec((tm, n_memq), row)),
        scratch_shapes=[pltpu.VMEM((tm, d), BF16)],
        compiler_params=_params("parallel"),
        name="mla_in",
    )(h, g.reshape(1, d), w_in, g_q.reshape(1, rank), w_uq_pad, ca, cb)


def _rope_tables(positions):
    half = MLA_ROPE_DIM // 2
    inv_freq = ROPE_THETA ** (-jnp.arange(half, dtype=F32) / half)
    ang = positions.astype(F32)[..., None] * inv_freq
    cos, sin = jnp.cos(ang), jnp.sin(ang)
    zeros = jnp.zeros(cos.shape[:-1] + (LANES - MLA_ROPE_DIM,), F32)
    ca = jnp.concatenate([cos, cos, zeros], axis=-1)
    cb = jnp.concatenate([-sin, sin, zeros], axis=-1)
    return ca.reshape(-1, LANES), cb.reshape(-1, LANES)


def kernel(x, mem, positions, attn_norm_g, ffn_norm_g, a_w_in, a_w_out, b_w_in, b_q_norm_g, b_w_uq, b_w_out, mem_norm_g, w_mem_kv, kv_norm_g, w_dkv, kv_latent_g, w_ukv, ffn_w_gu, ffn_w_down, final_norm_g):
    b, s, d = x.shape
    mem_len = mem.shape[1]
    depth = attn_norm_g.shape[0]
    n_a = a_w_in.shape[0]
    n_b = b_w_in.shape[0]
    mq_w = MEM_HEADS * HEAD_DIM
    sb_heads = (a_w_in.shape[2] - mq_w) // (3 * HEAD_DIM)
    q_rank = b_q_norm_g.shape[1]
    kv_rank = kv_latent_g.shape[0]
    mla_heads = b_w_uq.shape[2] // (MLA_NOPE_DIM + MLA_ROPE_DIM)

    a_w_in_b = a_w_in.astype(BF16)
    a_w_out_b = a_w_out.astype(BF16)
    b_w_in_b = b_w_in.astype(BF16)
    b_w_out_b = b_w_out.astype(BF16)
    w_gu_b = ffn_w_gu.astype(BF16)
    w_d_b = ffn_w_down.astype(BF16)
    w_ukv_b = w_ukv.astype(BF16)
    w_mem_all = jnp.transpose(w_mem_kv, (1, 0, 2)).reshape(d, depth * 2 * mq_w).astype(BF16)
    w_dkv_pad = jnp.pad(w_dkv, ((0, 0), (0, LANES - MLA_ROPE_DIM))).astype(BF16)
    w_uq_pad = jnp.pad(
        b_w_uq.reshape(n_b, q_rank, mla_heads, MLA_NOPE_DIM + MLA_ROPE_DIM),
        ((0, 0), (0, 0), (0, 0), (0, MLA_QK_PAD - MLA_NOPE_DIM - MLA_ROPE_DIM)),
    ).reshape(n_b, q_rank, mla_heads * MLA_QK_PAD).astype(BF16)
    ca, cb = _rope_tables(positions)

    h = x.reshape(b * s, d)
    mkv = _norm_matmul(mem.reshape(b * mem_len, d), mem_norm_g, w_mem_all, tm=1024, tn=1024)
    mkv = mkv.reshape(b, mem_len, depth * 2 * mq_w)
    kv_blocks = 2 * MEM_HEADS

    k_cat = v_lat = None
    for layer in range(depth):
        if layer == n_a:
            k_cat, v_lat = _latent_kv(h, kv_norm_g, w_dkv_pad, kv_latent_g, w_ukv_b, ca, cb,
                                      mla_heads, tm=512)
            k_cat = k_cat.reshape(b, s, -1)
            v_lat = v_lat.reshape(b, s, -1)
        if layer < n_a:
            proj = _norm_matmul(h, attn_norm_g[layer], a_w_in_b[layer], tm=1024, tn=1024)
            proj = proj.reshape(b, s, -1)
            mix = _sb_attention(proj, sb_heads, blk=256)
            memo = _mem_attention(proj, 3 * sb_heads, mkv, layer * kv_blocks, tq=2048)
            w_out = a_w_out_b[layer]
        else:
            li = layer - n_a
            q_cat, mem_q = _mla_in(h, attn_norm_g[layer], b_w_in_b[li], b_q_norm_g[li], w_uq_pad[li],
                                   ca, cb, mla_heads, tm=512)
            mix = _mla_attention(q_cat.reshape(b, s, -1), k_cat, v_lat, mla_heads, blk=256)
            memo = _mem_attention(mem_q.reshape(b, s, -1), 0, mkv, layer * kv_blocks, tq=2048)
            w_out = b_w_out_b[li]
        h = _out_proj(mix.reshape(b * s, -1), memo.reshape(b * s, -1), w_out, h, tm=512)
        h = _ffn(h, ffn_norm_g[layer], w_gu_b[layer], w_d_b[layer], final_norm_g,
                 final_norm=(layer == depth - 1), tm=512, tf=512)
    return h.reshape(b, s, d)
```

```python
import functools

import jax
import jax.numpy as jnp
from jax import lax
from jax.experimental import pallas as pl
from jax.experimental.pallas import tpu as pltpu

HEAD_DIM = 128
MEM_HEADS = 4
MLA_NOPE_DIM = 128
MLA_ROPE_DIM = 64
MLA_V_DIM = 128
ROPE_THETA = 10000.0
RMS_EPS = 1e-6
LOG2_E = 1.4426950408889634

LANES = 128
MLA_QK_PAD = 2 * LANES
VMEM_LIMIT_BYTES = 48 * 1024 * 1024
NORM_CHUNK_ROWS = 256

BF16 = jnp.bfloat16
F32 = jnp.float32


VMEM_LIMIT_BYTES_FFN = 58 * 1024 * 1024


def _params(*semantics, vmem_limit_bytes=VMEM_LIMIT_BYTES):
    return pltpu.CompilerParams(dimension_semantics=semantics, vmem_limit_bytes=vmem_limit_bytes)


def _tile(n, pref):
    t = min(n, pref)
    assert n % t == 0, (n, pref)
    return t


def _rms(x, g):
    return (x * lax.rsqrt(jnp.mean(x * x, axis=-1, keepdims=True) + RMS_EPS)) * g


def _rmsnorm_rows(x_ref, g_ref, out_ref):
    rows = x_ref.shape[0]
    chunk = min(rows, NORM_CHUNK_ROWS)
    g = g_ref[...]

    def body(c, carry):
        r = pl.multiple_of(c * chunk, chunk)
        out_ref[pl.ds(r, chunk), :] = _rms(x_ref[pl.ds(r, chunk), :], g).astype(out_ref.dtype)
        return carry

    lax.fori_loop(0, rows // chunk, body, 0)


def _rope(x, ca, cb):
    partner = pltpu.roll(x, 96, 1) + pltpu.roll(x, 32, 1)
    return x * ca + partner * cb


def _norm_matmul_kernel(h_ref, g_ref, w_ref, o_ref, xn_ref):
    @pl.when(pl.program_id(1) == 0)
    def _():
        _rmsnorm_rows(h_ref, g_ref, xn_ref)

    o_ref[...] = jnp.dot(xn_ref[...], w_ref[...], preferred_element_type=F32).astype(o_ref.dtype)


def _norm_matmul(h, g, w, *, tm, tn):
    m, d = h.shape
    n = w.shape[1]
    tm, tn = _tile(m, tm), _tile(n, tn)
    return pl.pallas_call(
        _norm_matmul_kernel,
        out_shape=jax.ShapeDtypeStruct((m, n), BF16),
        grid=(m // tm, n // tn),
        in_specs=[
            pl.BlockSpec((tm, d), lambda i, j: (i, 0)),
            pl.BlockSpec((1, d), lambda i, j: (0, 0)),
            pl.BlockSpec((d, tn), lambda i, j: (0, j)),
        ],
        out_specs=pl.BlockSpec((tm, tn), lambda i, j: (i, j)),
        scratch_shapes=[pltpu.VMEM((tm, d), BF16)],
        compiler_params=_params("parallel", "arbitrary"),
        name="norm_matmul",
    )(h, g.reshape(1, d), w)


def _ffn_kernel(h_ref, g_ref, wg_ref, wu_ref, wd_ref, gf_ref, o_ref, xn_ref, *, final_norm):
    j = pl.program_id(1)

    @pl.when(j == 0)
    def _():
        _rmsnorm_rows(h_ref, g_ref, xn_ref)
        o_ref[...] = h_ref[...]

    xn = xn_ref[...]
    gate = jnp.dot(xn, wg_ref[...], preferred_element_type=F32)
    up = jnp.dot(xn, wu_ref[...], preferred_element_type=F32)
    act = (gate * jax.nn.sigmoid(gate) * up).astype(BF16)
    o_ref[...] += jnp.dot(act, wd_ref[...], preferred_element_type=F32)

    if final_norm:
        @pl.when(j == pl.num_programs(1) - 1)
        def _():
            _rmsnorm_rows(o_ref, gf_ref, o_ref)


def _ffn(h, g, w_gu, w_d, gf, *, final_norm, tm, tf):
    m, d = h.shape
    f = w_d.shape[0]
    tm, tf = _tile(m, tm), _tile(f, tf)
    nf = f // tf
    return pl.pallas_call(
        functools.partial(_ffn_kernel, final_norm=final_norm),
        out_shape=jax.ShapeDtypeStruct((m, d), F32),
        grid=(m // tm, nf),
        in_specs=[
            pl.BlockSpec((tm, d), lambda i, j: (i, 0), pipeline_mode=pl.Buffered(1)),
            pl.BlockSpec((1, d), lambda i, j: (0, 0)),
            pl.BlockSpec((d, tf), lambda i, j: (0, j)),
            pl.BlockSpec((d, tf), lambda i, j: (0, nf + j)),
            pl.BlockSpec((tf, d), lambda i, j: (j, 0)),
            pl.BlockSpec((1, d), lambda i, j: (0, 0)),
        ],
        out_specs=pl.BlockSpec((tm, d), lambda i, j: (i, 0)),
        scratch_shapes=[pltpu.VMEM((tm, d), BF16)],
        compiler_params=_params("parallel", "arbitrary", vmem_limit_bytes=VMEM_LIMIT_BYTES_FFN),
        name="ffn",
    )(h, g.reshape(1, d), w_gu, w_gu, w_d, gf.reshape(1, d))


def _out_proj_kernel(mix_ref, memo_ref, w1_ref, w2_ref, h_ref, o_ref):
    o_ref[...] = (h_ref[...]
                  + jnp.dot(mix_ref[...], w1_ref[...], preferred_element_type=F32)
                  + jnp.dot(memo_ref[...], w2_ref[...], preferred_element_type=F32))


def _out_proj(mix, memo, w_out, h, *, tm):
    m, d = h.shape
    k1, k2 = mix.shape[1], memo.shape[1]
    tm = _tile(m, tm)
    return pl.pallas_call(
        _out_proj_kernel,
        out_shape=jax.ShapeDtypeStruct((m, d), F32),
        grid=(m // tm,),
        in_specs=[
            pl.BlockSpec((tm, k1), lambda i: (i, 0)),
            pl.BlockSpec((tm, k2), lambda i: (i, 0)),
            pl.BlockSpec((k1, d), lambda i: (0, 0)),
            pl.BlockSpec((k2, d), lambda i: (0, 0)),
            pl.BlockSpec((tm, d), lambda i: (i, 0)),
        ],
        out_specs=pl.BlockSpec((tm, d), lambda i: (i, 0)),
        compiler_params=_params("parallel"),
        name="out_proj",
    )(mix, memo, w_out[:k1], w_out[k1:], h)


def _sb_attn_kernel(q_ref, k_ref, v_ref, tri_ref, o_ref, *, blk, scale):
    tri2 = tri_ref[...]
    row = lax.broadcasted_iota(jnp.int32, (blk, blk), 0)
    col = lax.broadcasted_iota(jnp.int32, (blk, blk), 1)
    strictly_past = col < row
    nq = q_ref.shape[0] // blk

    def logits(i):
        q = q_ref[i * blk:(i + 1) * blk, :]
        return lax.dot_general(q, k_ref[0:(i + 1) * blk, :], (((1,), (1,)), ((), ())),
                               preferred_element_type=F32) * (-scale * LOG2_E)

    def log_keep(nz):
        neg_abs = lax.bitcast_convert_type(
            lax.bitcast_convert_type(nz, jnp.uint32) | jnp.uint32(0x80000000), F32)
        return jnp.minimum(nz, 0.0) - jnp.log2(1.0 + jnp.exp2(neg_abs))

    def finish(i, nz, lkeep):
        w_blocks = [None] * (i + 1)
        carry = None
        for j in range(i, -1, -1):
            nzj = nz[:, j * blk:(j + 1) * blk]
            lk = lkeep[:, j * blk:(j + 1) * blk]
            if j == i:
                lk = jnp.where(strictly_past, lk, 0.0)
            hi = lk.astype(BF16)
            lo = (lk - hi.astype(F32)).astype(BF16)
            log_w = jnp.dot(jnp.concatenate([hi, lo], axis=1), tri2, preferred_element_type=F32) - nzj
            if carry is not None:
                log_w = log_w + carry
            w = jnp.exp2(log_w)
            if j == i:
                w = jnp.where(strictly_past, w, 0.0)
            w_blocks[j] = w.astype(BF16)
            block_sum = jnp.sum(lk, axis=1, keepdims=True)
            carry = block_sum if carry is None else carry + block_sum
        w = w_blocks[0] if i == 0 else jnp.concatenate(w_blocks, axis=1)
        o_ref[i * blk:(i + 1) * blk, :] = jnp.dot(
            w, v_ref[0:(i + 1) * blk, :], preferred_element_type=F32).astype(o_ref.dtype)

    order = list(range(nq - 1, -1, -1))
    nz_of, lk_of = {}, {}
    for step in range(nq + 2):
        if step < nq:
            nz_of[order[step]] = logits(order[step])
        if 0 <= step - 1 < nq:
            lk_of[order[step - 1]] = log_keep(nz_of[order[step - 1]])
        if 0 <= step - 2 < nq:
            i = order[step - 2]
            finish(i, nz_of.pop(i), lk_of.pop(i))


def _sb_attention(proj, n_heads, *, blk):
    b, s, _ = proj.shape
    blk = _tile(s, blk)
    tri = (jnp.arange(blk)[:, None] >= jnp.arange(blk)[None, :]).astype(BF16)
    tri2 = jnp.concatenate([tri, tri], axis=0)
    return pl.pallas_call(
        functools.partial(_sb_attn_kernel, blk=blk, scale=HEAD_DIM ** -0.5),
        out_shape=jax.ShapeDtypeStruct((b, s, n_heads * HEAD_DIM), BF16),
        grid=(b, n_heads),
        in_specs=[
            pl.BlockSpec((None, s, HEAD_DIM), lambda bi, h: (bi, 0, h)),
            pl.BlockSpec((None, s, HEAD_DIM), lambda bi, h: (bi, 0, n_heads + h)),
            pl.BlockSpec((None, s, HEAD_DIM), lambda bi, h: (bi, 0, 2 * n_heads + h)),
            pl.BlockSpec((2 * blk, blk), lambda bi, h: (0, 0)),
        ],
        out_specs=pl.BlockSpec((None, s, HEAD_DIM), lambda bi, h: (bi, 0, h)),
        compiler_params=_params("parallel", "parallel"),
        name="sb_attention",
    )(proj, proj, proj, tri2)


def _mla_attn_kernel(q_ref, k_ref, v_ref, o_ref, *, blk, scale):
    row = lax.broadcasted_iota(jnp.int32, (blk, blk), 0)
    col = lax.broadcasted_iota(jnp.int32, (blk, blk), 1)
    causal = col <= row
    for i in range(q_ref.shape[0] // blk):
        n = (i + 1) * blk
        q = q_ref[i * blk:n, :]
        s = lax.dot_general(q, k_ref[0:n, :], (((1,), (1,)), ((), ())),
                            preferred_element_type=F32)
        diag = jnp.where(causal, s[:, i * blk:n], -jnp.inf)
        s = diag if i == 0 else jnp.concatenate([s[:, :i * blk], diag], axis=1)
        m = jnp.max(s, axis=1, keepdims=True)
        p = jnp.exp2((s - m) * (scale * LOG2_E))
        l = jnp.sum(p, axis=1, keepdims=True)
        o = jnp.dot(p.astype(BF16), v_ref[0:n, :], preferred_element_type=F32)
        o_ref[i * blk:n, :] = (o / l).astype(o_ref.dtype)


def _mla_attention(q_cat, k_cat, v, n_heads, *, blk):
    b, s, _ = q_cat.shape
    blk = _tile(s, blk)
    scale = (MLA_NOPE_DIM + MLA_ROPE_DIM) ** -0.5
    return pl.pallas_call(
        functools.partial(_mla_attn_kernel, blk=blk, scale=scale),
        out_shape=jax.ShapeDtypeStruct((b, s, n_heads * MLA_V_DIM), BF16),
        grid=(b, n_heads),
        in_specs=[
            pl.BlockSpec((None, s, MLA_QK_PAD), lambda bi, h: (bi, 0, h)),
            pl.BlockSpec((None, s, MLA_QK_PAD), lambda bi, h: (bi, 0, h)),
            pl.BlockSpec((None, s, MLA_V_DIM), lambda bi, h: (bi, 0, h)),
        ],
        out_specs=pl.BlockSpec((None, s, MLA_V_DIM), lambda bi, h: (bi, 0, h)),
        compiler_params=_params("parallel", "parallel"),
        name="mla_attention",
    )(q_cat, k_cat, v)


def _mem_attn_kernel(q_ref, k_ref, v_ref, o_ref, *, scale):
    s = lax.dot_general(q_ref[...], k_ref[...], (((1,), (1,)), ((), ())),
                        preferred_element_type=F32)
    m = jnp.max(s, axis=1, keepdims=True)
    p = jnp.exp2((s - m) * (scale * LOG2_E))
    l = jnp.sum(p, axis=1, keepdims=True)
    o = jnp.dot(p.astype(BF16), v_ref[...], preferred_element_type=F32)
    o_ref[...] = (o / l).astype(o_ref.dtype)


def _mem_attention(q_src, q_col0, mkv, kv_col0, *, tq):
    b, s, _ = q_src.shape
    ml = mkv.shape[1]
    tq = _tile(s, tq)
    return pl.pallas_call(
        functools.partial(_mem_attn_kernel, scale=HEAD_DIM ** -0.5),
        out_shape=jax.ShapeDtypeStruct((b, s, MEM_HEADS * HEAD_DIM), BF16),
        grid=(b, MEM_HEADS, s // tq),
        in_specs=[
            pl.BlockSpec((None, tq, HEAD_DIM), lambda bi, h, i: (bi, i, q_col0 + h)),
            pl.BlockSpec((None, ml, HEAD_DIM), lambda bi, h, i: (bi, 0, kv_col0 + h)),
            pl.BlockSpec((None, ml, HEAD_DIM), lambda bi, h, i: (bi, 0, kv_col0 + MEM_HEADS + h)),
        ],
        out_specs=pl.BlockSpec((None, tq, HEAD_DIM), lambda bi, h, i: (bi, i, h)),
        compiler_params=_params("parallel", "parallel", "parallel"),
        name="mem_attention",
    )(q_src, mkv, mkv)


def _latent_kv_kernel(h_ref, g_ref, wdkv_ref, gl_ref, wukv_ref, ca_ref, cb_ref,
                      kcat_ref, v_ref, xn_ref, *, n_heads, rank):
    _rmsnorm_rows(h_ref, g_ref, xn_ref)
    ckv = jnp.dot(xn_ref[...], wdkv_ref[...], preferred_element_type=F32)
    c_latent = _rms(ckv[:, :rank], gl_ref[...]).astype(BF16)
    k_rope = _rope(ckv[:, rank:], ca_ref[...], cb_ref[...]).astype(BF16)
    for h in range(n_heads):
        kv = jnp.dot(c_latent, wukv_ref[:, h * 256:(h + 1) * 256], preferred_element_type=F32)
        kcat_ref[:, h * MLA_QK_PAD:h * MLA_QK_PAD + MLA_NOPE_DIM] = kv[:, :MLA_NOPE_DIM].astype(BF16)
        kcat_ref[:, h * MLA_QK_PAD + MLA_NOPE_DIM:(h + 1) * MLA_QK_PAD] = k_rope
        v_ref[:, h * MLA_V_DIM:(h + 1) * MLA_V_DIM] = kv[:, MLA_NOPE_DIM:].astype(BF16)


def _latent_kv(h, g, w_dkv_pad, g_latent, w_ukv, ca, cb, n_heads, *, tm):
    m, d = h.shape
    rank = g_latent.shape[0]
    tm = _tile(m, tm)
    row = lambda i: (i, 0)
    fixed = lambda i: (0, 0)
    return pl.pallas_call(
        functools.partial(_latent_kv_kernel, n_heads=n_heads, rank=rank),
        out_shape=(jax.ShapeDtypeStruct((m, n_heads * MLA_QK_PAD), BF16),
                   jax.ShapeDtypeStruct((m, n_heads * MLA_V_DIM), BF16)),
        grid=(m // tm,),
        in_specs=[
            pl.BlockSpec((tm, d), row),
            pl.BlockSpec((1, d), fixed),
            pl.BlockSpec(w_dkv_pad.shape, fixed),
            pl.BlockSpec((1, rank), fixed),
            pl.BlockSpec(w_ukv.shape, fixed),
            pl.BlockSpec((tm, LANES), row),
            pl.BlockSpec((tm, LANES), row),
        ],
        out_specs=(pl.BlockSpec((tm, n_heads * MLA_QK_PAD), row),
                   pl.BlockSpec((tm, n_heads * MLA_V_DIM), row)),
        scratch_shapes=[pltpu.VMEM((tm, d), BF16)],
        compiler_params=_params("parallel"),
        name="latent_kv",
    )(h, g.reshape(1, d), w_dkv_pad, g_latent.reshape(1, rank), w_ukv, ca, cb)


def _mla_in_kernel(h_ref, g_ref, win_ref, gq_ref, wuq_ref, ca_ref, cb_ref,
                   qcat_ref, memq_ref, xn_ref, *, n_heads, rank):
    _rmsnorm_rows(h_ref, g_ref, xn_ref)
    proj = jnp.dot(xn_ref[...], win_ref[...], preferred_element_type=F32)
    memq_ref[...] = proj[:, rank:].astype(BF16)
    c_q = _rms(proj[:, :rank], gq_ref[...]).astype(BF16)
    ca, cb = ca_ref[...], cb_ref[...]
    for h in range(n_heads):
        q = jnp.dot(c_q, wuq_ref[:, h * MLA_QK_PAD:(h + 1) * MLA_QK_PAD], preferred_element_type=F32)
        qcat_ref[:, h * MLA_QK_PAD:h * MLA_QK_PAD + MLA_NOPE_DIM] = q[:, :MLA_NOPE_DIM].astype(BF16)
        qcat_ref[:, h * MLA_QK_PAD + MLA_NOPE_DIM:(h + 1) * MLA_QK_PAD] = (
            _rope(q[:, MLA_NOPE_DIM:], ca, cb).astype(BF16))


def _mla_in(h, g, w_in, g_q, w_uq_pad, ca, cb, n_heads, *, tm):
    m, d = h.shape
    rank = g_q.shape[0]
    n_memq = w_in.shape[1] - rank
    tm = _tile(m, tm)
    row = lambda i: (i, 0)
    fixed = lambda i: (0, 0)
    return pl.pallas_call(
        functools.partial(_mla_in_kernel, n_heads=n_heads, rank=rank),
        out_shape=(jax.ShapeDtypeStruct((m, n_heads * MLA_QK_PAD), BF16),
                   jax.ShapeDtypeStruct((m, n_memq), BF16)),
        grid=(m // tm,),
        in_specs=[
            pl.BlockSpec((tm, d), row),
            pl.BlockSpec((1, d), fixed),
            pl.BlockSpec(w_in.shape, fixed),
            pl.BlockSpec((1, rank), fixed),
            pl.BlockSpec(w_uq_pad.shape, fixed),
            pl.BlockSpec((tm, LANES), row),
            pl.BlockSpec((tm, LANES), row),
        ],
        out_specs=(pl.BlockSpec((tm, n_heads * MLA_QK_PAD), row),
                   pl.BlockSpec((tm, n_memq), row)),
        scratch_shapes=[pltpu.VMEM((tm, d), BF16)],
        compiler_params=_params("parallel"),
        name="mla_in",
    )(h, g.reshape(1, d), w_in, g_q.reshape(1, rank), w_uq_pad, ca, cb)


def _rope_tables(positions):
    half = MLA_ROPE_DIM // 2
    inv_freq = ROPE_THETA ** (-jnp.arange(half, dtype=F32) / half)
    ang = positions.astype(F32)[..., None] * inv_freq
    cos, sin = jnp.cos(ang), jnp.sin(ang)
    zeros = jnp.zeros(cos.shape[:-1] + (LANES - MLA_ROPE_DIM,), F32)
    ca = jnp.concatenate([cos, cos, zeros], axis=-1)
    cb = jnp.concatenate([-sin, sin, zeros], axis=-1)
    return ca.reshape(-1, LANES), cb.reshape(-1, LANES)


def kernel(x, mem, positions, attn_norm_g, ffn_norm_g, a_w_in, a_w_out, b_w_in, b_q_norm_g, b_w_uq, b_w_out, mem_norm_g, w_mem_kv, kv_norm_g, w_dkv, kv_latent_g, w_ukv, ffn_w_gu, ffn_w_down, final_norm_g):
    b, s, d = x.shape
    mem_len = mem.shape[1]
    depth = attn_norm_g.shape[0]
    n_a = a_w_in.shape[0]
    n_b = b_w_in.shape[0]
    mq_w = MEM_HEADS * HEAD_DIM
    sb_heads = (a_w_in.shape[2] - mq_w) // (3 * HEAD_DIM)
    q_rank = b_q_norm_g.shape[1]
    kv_rank = kv_latent_g.shape[0]
    mla_heads = b_w_uq.shape[2] // (MLA_NOPE_DIM + MLA_ROPE_DIM)

    a_w_in_b = a_w_in.astype(BF16)
    a_w_out_b = a_w_out.astype(BF16)
    b_w_in_b = b_w_in.astype(BF16)
    b_w_out_b = b_w_out.astype(BF16)
    w_gu_b = ffn_w_gu.astype(BF16)
    w_d_b = ffn_w_down.astype(BF16)
    w_ukv_b = w_ukv.astype(BF16)
    w_mem_all = jnp.transpose(w_mem_kv, (1, 0, 2)).reshape(d, depth * 2 * mq_w).astype(BF16)
    w_dkv_pad = jnp.pad(w_dkv, ((0, 0), (0, LANES - MLA_ROPE_DIM))).astype(BF16)
    w_uq_pad = jnp.pad(
        b_w_uq.reshape(n_b, q_rank, mla_heads, MLA_NOPE_DIM + MLA_ROPE_DIM),
        ((0, 0), (0, 0), (0, 0), (0, MLA_QK_PAD - MLA_NOPE_DIM - MLA_ROPE_DIM)),
    ).reshape(n_b, q_rank, mla_heads * MLA_QK_PAD).astype(BF16)
    ca, cb = _rope_tables(positions)

    h = x.reshape(b * s, d)
    mkv = _norm_matmul(mem.reshape(b * mem_len, d), mem_norm_g, w_mem_all, tm=1024, tn=1024)
    mkv = mkv.reshape(b, mem_len, depth * 2 * mq_w)
    kv_blocks = 2 * MEM_HEADS

    k_cat = v_lat = None
    for layer in range(depth):
        if layer == n_a:
            k_cat, v_lat = _latent_kv(h, kv_norm_g, w_dkv_pad, kv_latent_g, w_ukv_b, ca, cb,
                                      mla_heads, tm=512)
            k_cat = k_cat.reshape(b, s, -1)
            v_lat = v_lat.reshape(b, s, -1)
        if layer < n_a:
            proj = _norm_matmul(h, attn_norm_g[layer], a_w_in_b[layer], tm=1024, tn=1024)
            proj = proj.reshape(b, s, -1)
            mix = _sb_attention(proj, sb_heads, blk=256)
            memo = _mem_attention(proj, 3 * sb_heads, mkv, layer * kv_blocks, tq=2048)
            w_out = a_w_out_b[layer]
        else:
            li = layer - n_a
            q_cat, mem_q = _mla_in(h, attn_norm_g[layer], b_w_in_b[li], b_q_norm_g[li], w_uq_pad[li],
                                   ca, cb, mla_heads, tm=512)
            mix = _mla_attention(q_cat.reshape(b, s, -1), k_cat, v_lat, mla_heads, blk=256)
            memo = _mem_attention(mem_q.reshape(b, s, -1), 0, mkv, layer * kv_blocks, tq=2048)
            w_out = b_w_out_b[li]
        h = _out_proj(mix.reshape(b * s, -1), memo.reshape(b * s, -1), w_out, h, tm=512)
        h = _ffn(h, ffn_norm_g[layer], w_gu_b[layer], w_d_b[layer], final_norm_g,
                 final_norm=(layer == depth - 1), tm=1024, tf=512)
    return h.reshape(b, s, d)
```

```python
import functools

import jax
import jax.numpy as jnp
from jax import lax
from jax.experimental import pallas as pl
from jax.experimental.pallas import tpu as pltpu

HEAD_DIM = 128
MEM_HEADS = 4
MLA_NOPE_DIM = 128
MLA_ROPE_DIM = 64
MLA_V_DIM = 128
ROPE_THETA = 10000.0
RMS_EPS = 1e-6
LOG2_E = 1.4426950408889634

LANES = 128
MLA_QK_PAD = 2 * LANES
VMEM_LIMIT_BYTES = 48 * 1024 * 1024
NORM_CHUNK_ROWS = 256
FFN_FIRST_STEP_SPLITS = 2
NORM_MATMUL_FIRST_STEP_SPLITS = 2
ROW_SPLITS = 2

BF16 = jnp.bfloat16
F32 = jnp.float32


VMEM_LIMIT_BYTES_FFN = 58 * 1024 * 1024


def _params(*semantics, vmem_limit_bytes=VMEM_LIMIT_BYTES):
    return pltpu.CompilerParams(dimension_semantics=semantics, vmem_limit_bytes=vmem_limit_bytes)


def _tile(n, pref):
    t = min(n, pref)
    assert n % t == 0, (n, pref)
    return t


def _rms(x, g):
    return (x * lax.rsqrt(jnp.mean(x * x, axis=-1, keepdims=True) + RMS_EPS)) * g


def _rmsnorm_rows(x_ref, g_ref, out_ref):
    rows = x_ref.shape[0]
    chunk = min(rows, NORM_CHUNK_ROWS)
    g = g_ref[...]

    def body(c, carry):
        r = pl.multiple_of(c * chunk, chunk)
        out_ref[pl.ds(r, chunk), :] = _rms(x_ref[pl.ds(r, chunk), :], g).astype(out_ref.dtype)
        return carry

    lax.fori_loop(0, rows // chunk, body, 0)


def _rope(x, ca, cb):
    partner = pltpu.roll(x, 96, 1) + pltpu.roll(x, 32, 1)
    return x * ca + partner * cb


def _norm_matmul_kernel(h_ref, g_ref, w_ref, o_ref, xn_ref):
    j = pl.program_id(1)

    @pl.when(j == 0)
    def _():
        rows = h_ref.shape[0] // NORM_MATMUL_FIRST_STEP_SPLITS
        g = g_ref[...]
        for c in range(NORM_MATMUL_FIRST_STEP_SPLITS):
            xn = _rms(h_ref[c * rows:(c + 1) * rows, :], g).astype(BF16)
            xn_ref[c * rows:(c + 1) * rows, :] = xn
            o_ref[c * rows:(c + 1) * rows, :] = jnp.dot(
                xn, w_ref[...], preferred_element_type=F32).astype(o_ref.dtype)

    @pl.when(j > 0)
    def _():
        o_ref[...] = jnp.dot(xn_ref[...], w_ref[...], preferred_element_type=F32).astype(o_ref.dtype)


def _norm_matmul(h, g, w, *, tm, tn):
    m, d = h.shape
    n = w.shape[1]
    tm, tn = _tile(m, tm), _tile(n, tn)
    return pl.pallas_call(
        _norm_matmul_kernel,
        out_shape=jax.ShapeDtypeStruct((m, n), BF16),
        grid=(m // tm, n // tn),
        in_specs=[
            pl.BlockSpec((tm, d), lambda i, j: (i, 0)),
            pl.BlockSpec((1, d), lambda i, j: (0, 0)),
            pl.BlockSpec((d, tn), lambda i, j: (0, j)),
        ],
        out_specs=pl.BlockSpec((tm, tn), lambda i, j: (i, j)),
        scratch_shapes=[pltpu.VMEM((tm, d), BF16)],
        compiler_params=_params("parallel", "arbitrary"),
        name="norm_matmul",
    )(h, g.reshape(1, d), w)


def _ffn_kernel(h_ref, g_ref, wg_ref, wu_ref, wd_ref, gf_ref, o_ref, xn_ref, *, final_norm):
    j = pl.program_id(1)

    def hidden_tile(xn):
        gate = jnp.dot(xn, wg_ref[...], preferred_element_type=F32)
        up = jnp.dot(xn, wu_ref[...], preferred_element_type=F32)
        act = (gate * jax.nn.sigmoid(gate) * up).astype(BF16)
        return jnp.dot(act, wd_ref[...], preferred_element_type=F32)

    @pl.when(j == 0)
    def _():
        half = h_ref.shape[0] // FFN_FIRST_STEP_SPLITS
        g = g_ref[...]
        for c in range(FFN_FIRST_STEP_SPLITS):
            h = h_ref[c * half:(c + 1) * half, :]
            xn = _rms(h, g).astype(BF16)
            xn_ref[c * half:(c + 1) * half, :] = xn
            o_ref[c * half:(c + 1) * half, :] = h + hidden_tile(xn)

    @pl.when(j > 0)
    def _():
        o_ref[...] += hidden_tile(xn_ref[...])

    if final_norm:
        @pl.when(j == pl.num_programs(1) - 1)
        def _():
            _rmsnorm_rows(o_ref, gf_ref, o_ref)


def _ffn(h, g, w_gu, w_d, gf, *, final_norm, tm, tf):
    m, d = h.shape
    f = w_d.shape[0]
    tm, tf = _tile(m, tm), _tile(f, tf)
    nf = f // tf
    return pl.pallas_call(
        functools.partial(_ffn_kernel, final_norm=final_norm),
        out_shape=jax.ShapeDtypeStruct((m, d), F32),
        grid=(m // tm, nf),
        in_specs=[
            pl.BlockSpec((tm, d), lambda i, j: (i, 0)),
            pl.BlockSpec((1, d), lambda i, j: (0, 0)),
            pl.BlockSpec((d, tf), lambda i, j: (0, j)),
            pl.BlockSpec((d, tf), lambda i, j: (0, nf + j)),
            pl.BlockSpec((tf, d), lambda i, j: (j, 0)),
            pl.BlockSpec((1, d), lambda i, j: (0, 0)),
        ],
        out_specs=pl.BlockSpec((tm, d), lambda i, j: (i, 0)),
        scratch_shapes=[pltpu.VMEM((tm, d), BF16)],
        compiler_params=_params("parallel", "arbitrary", vmem_limit_bytes=VMEM_LIMIT_BYTES_FFN),
        name="ffn",
    )(h, g.reshape(1, d), w_gu, w_gu, w_d, gf.reshape(1, d))


def _out_proj_kernel(mix_ref, memo_ref, w1_ref, w2_ref, h_ref, o_ref):
    o_ref[...] = (h_ref[...]
                  + jnp.dot(mix_ref[...], w1_ref[...], preferred_element_type=F32)
                  + jnp.dot(memo_ref[...], w2_ref[...], preferred_element_type=F32))


def _out_proj(mix, memo, w_out, h, *, tm):
    m, d = h.shape
    k1, k2 = mix.shape[1], memo.shape[1]
    assert k1 % k2 == 0 and w_out.shape[0] == k1 + k2
    tm = _tile(m, tm)
    return pl.pallas_call(
        _out_proj_kernel,
        out_shape=jax.ShapeDtypeStruct((m, d), F32),
        grid=(m // tm,),
        in_specs=[
            pl.BlockSpec((tm, k1), lambda i: (i, 0)),
            pl.BlockSpec((tm, k2), lambda i: (i, 0)),
            pl.BlockSpec((k1, d), lambda i: (0, 0)),
            pl.BlockSpec((k2, d), lambda i: (k1 // k2, 0)),
            pl.BlockSpec((tm, d), lambda i: (i, 0)),
        ],
        out_specs=pl.BlockSpec((tm, d), lambda i: (i, 0)),
        compiler_params=_params("parallel"),
        name="out_proj",
    )(mix, memo, w_out, w_out, h)


def _sb_attn_kernel(q_ref, k_ref, v_ref, tri_ref, o_ref, *, blk, scale):
    tri2 = tri_ref[...]
    row = lax.broadcasted_iota(jnp.int32, (blk, blk), 0)
    col = lax.broadcasted_iota(jnp.int32, (blk, blk), 1)
    strictly_past = col < row
    nq = q_ref.shape[0] // blk

    def logits(i):
        q = q_ref[i * blk:(i + 1) * blk, :]
        return lax.dot_general(q, k_ref[0:(i + 1) * blk, :], (((1,), (1,)), ((), ())),
                               preferred_element_type=F32) * (-scale * LOG2_E)

    def log_keep(nz):
        neg_abs = lax.bitcast_convert_type(
            lax.bitcast_convert_type(nz, jnp.uint32) | jnp.uint32(0x80000000), F32)
        return jnp.minimum(nz, 0.0) - jnp.log2(1.0 + jnp.exp2(neg_abs))

    def finish(i, nz, lkeep):
        w_blocks = [None] * (i + 1)
        carry = None
        for j in range(i, -1, -1):
            nzj = nz[:, j * blk:(j + 1) * blk]
            lk = lkeep[:, j * blk:(j + 1) * blk]
            if j == i:
                lk = jnp.where(strictly_past, lk, 0.0)
            hi = lk.astype(BF16)
            lo = (lk - hi.astype(F32)).astype(BF16)
            log_w = jnp.dot(jnp.concatenate([hi, lo], axis=1), tri2, preferred_element_type=F32) - nzj
            if carry is not None:
                log_w = log_w + carry
            w = jnp.exp2(log_w)
            if j == i:
                w = jnp.where(strictly_past, w, 0.0)
            w_blocks[j] = w.astype(BF16)
            block_sum = jnp.sum(lk, axis=1, keepdims=True)
            carry = block_sum if carry is None else carry + block_sum
        w = w_blocks[0] if i == 0 else jnp.concatenate(w_blocks, axis=1)
        o_ref[i * blk:(i + 1) * blk, :] = jnp.dot(
            w, v_ref[0:(i + 1) * blk, :], preferred_element_type=F32).astype(o_ref.dtype)

    order = list(range(nq - 1, -1, -1))
    nz_of, lk_of = {}, {}
    for step in range(nq + 2):
        if step < nq:
            nz_of[order[step]] = logits(order[step])
        if 0 <= step - 1 < nq:
            lk_of[order[step - 1]] = log_keep(nz_of[order[step - 1]])
        if 0 <= step - 2 < nq:
            i = order[step - 2]
            finish(i, nz_of.pop(i), lk_of.pop(i))


def _sb_attention(proj, n_heads, *, blk):
    b, s, _ = proj.shape
    blk = _tile(s, blk)
    tri = (jnp.arange(blk)[:, None] >= jnp.arange(blk)[None, :]).astype(BF16)
    tri2 = jnp.concatenate([tri, tri], axis=0)
    return pl.pallas_call(
        functools.partial(_sb_attn_kernel, blk=blk, scale=HEAD_DIM ** -0.5),
        out_shape=jax.ShapeDtypeStruct((b, s, n_heads * HEAD_DIM), BF16),
        grid=(b, n_heads),
        in_specs=[
            pl.BlockSpec((None, s, HEAD_DIM), lambda bi, h: (bi, 0, h)),
            pl.BlockSpec((None, s, HEAD_DIM), lambda bi, h: (bi, 0, n_heads + h)),
            pl.BlockSpec((None, s, HEAD_DIM), lambda bi, h: (bi, 0, 2 * n_heads + h)),
            pl.BlockSpec((2 * blk, blk), lambda bi, h: (0, 0)),
        ],
        out_specs=pl.BlockSpec((None, s, HEAD_DIM), lambda bi, h: (bi, 0, h)),
        compiler_params=_params("parallel", "parallel"),
        name="sb_attention",
    )(proj, proj, proj, tri2)


def _mla_attn_kernel(q_ref, k_ref, v_ref, o_ref, *, blk, scale):
    row = lax.broadcasted_iota(jnp.int32, (blk, blk), 0)
    col = lax.broadcasted_iota(jnp.int32, (blk, blk), 1)
    causal = col <= row
    nq = q_ref.shape[0] // blk

    def scores(i):
        q = q_ref[i * blk:(i + 1) * blk, :]
        return lax.dot_general(q, k_ref[0:(i + 1) * blk, :], (((1,), (1,)), ((), ())),
                               preferred_element_type=F32)

    def probs(i, s):
        diag = jnp.where(causal, s[:, i * blk:], -jnp.inf)
        s = diag if i == 0 else jnp.concatenate([s[:, :i * blk], diag], axis=1)
        m = jnp.max(s, axis=1, keepdims=True)
        p = jnp.exp2((s - m) * (scale * LOG2_E))
        return p.astype(BF16), jnp.sum(p, axis=1, keepdims=True)

    def finish(i, p, l):
        o = jnp.dot(p, v_ref[0:(i + 1) * blk, :], preferred_element_type=F32)
        o_ref[i * blk:(i + 1) * blk, :] = (o / l).astype(o_ref.dtype)

    order = list(range(nq - 1, -1, -1))
    s_of, p_of = {}, {}
    for step in range(nq + 2):
        if step < nq:
            s_of[order[step]] = scores(order[step])
        if 0 <= step - 1 < nq:
            i = order[step - 1]
            p_of[i] = probs(i, s_of.pop(i))
        if 0 <= step - 2 < nq:
            i = order[step - 2]
            finish(i, *p_of.pop(i))


def _mla_attention(q_cat, k_cat, v, n_heads, *, blk):
    b, s, _ = q_cat.shape
    blk = _tile(s, blk)
    scale = (MLA_NOPE_DIM + MLA_ROPE_DIM) ** -0.5
    return pl.pallas_call(
        functools.partial(_mla_attn_kernel, blk=blk, scale=scale),
        out_shape=jax.ShapeDtypeStruct((b, s, n_heads * MLA_V_DIM), BF16),
        grid=(b, n_heads),
        in_specs=[
            pl.BlockSpec((None, s, MLA_QK_PAD), lambda bi, h: (bi, 0, h)),
            pl.BlockSpec((None, s, MLA_QK_PAD), lambda bi, h: (bi, 0, h)),
            pl.BlockSpec((None, s, MLA_V_DIM), lambda bi, h: (bi, 0, h)),
        ],
        out_specs=pl.BlockSpec((None, s, MLA_V_DIM), lambda bi, h: (bi, 0, h)),
        compiler_params=_params("parallel", "parallel"),
        name="mla_attention",
    )(q_cat, k_cat, v)


def _mem_attn_kernel(q_ref, k_ref, v_ref, o_ref, *, scale):
    s = lax.dot_general(q_ref[...], k_ref[...], (((1,), (1,)), ((), ())),
                        preferred_element_type=F32)
    m = jnp.max(s, axis=1, keepdims=True)
    p = jnp.exp2((s - m) * (scale * LOG2_E))
    l = jnp.sum(p, axis=1, keepdims=True)
    o = jnp.dot(p.astype(BF16), v_ref[...], preferred_element_type=F32)
    o_ref[...] = (o / l).astype(o_ref.dtype)


def _mem_attention(q_src, q_col0, mkv, kv_col0, *, tq):
    b, s, _ = q_src.shape
    ml = mkv.shape[1]
    tq = _tile(s, tq)
    return pl.pallas_call(
        functools.partial(_mem_attn_kernel, scale=HEAD_DIM ** -0.5),
        out_shape=jax.ShapeDtypeStruct((b, s, MEM_HEADS * HEAD_DIM), BF16),
        grid=(b, MEM_HEADS, s // tq),
        in_specs=[
            pl.BlockSpec((None, tq, HEAD_DIM), lambda bi, h, i: (bi, i, q_col0 + h)),
            pl.BlockSpec((None, ml, HEAD_DIM), lambda bi, h, i: (bi, 0, kv_col0 + h)),
            pl.BlockSpec((None, ml, HEAD_DIM), lambda bi, h, i: (bi, 0, kv_col0 + MEM_HEADS + h)),
        ],
        out_specs=pl.BlockSpec((None, tq, HEAD_DIM), lambda bi, h, i: (bi, i, h)),
        compiler_params=_params("parallel", "parallel", "parallel"),
        name="mem_attention",
    )(q_src, mkv, mkv)


def _latent_kv_kernel(h_ref, g_ref, wdkv_ref, gl_ref, wukv_ref, ca_ref, cb_ref,
                      kcat_ref, v_ref, *, n_heads, rank):
    rows = h_ref.shape[0] // ROW_SPLITS
    for c in range(ROW_SPLITS):
        r = slice(c * rows, (c + 1) * rows)
        xn = _rms(h_ref[r, :], g_ref[...]).astype(BF16)
        ckv = jnp.dot(xn, wdkv_ref[...], preferred_element_type=F32)
        c_latent = _rms(ckv[:, :rank], gl_ref[...]).astype(BF16)
        k_rope = _rope(ckv[:, rank:], ca_ref[r, :], cb_ref[r, :]).astype(BF16)
        for h in range(n_heads):
            kv = jnp.dot(c_latent, wukv_ref[:, h * 256:(h + 1) * 256], preferred_element_type=F32)
            kcat_ref[r, h * MLA_QK_PAD:h * MLA_QK_PAD + MLA_NOPE_DIM] = kv[:, :MLA_NOPE_DIM].astype(BF16)
            kcat_ref[r, h * MLA_QK_PAD + MLA_NOPE_DIM:(h + 1) * MLA_QK_PAD] = k_rope
            v_ref[r, h * MLA_V_DIM:(h + 1) * MLA_V_DIM] = kv[:, MLA_NOPE_DIM:].astype(BF16)


def _latent_kv(h, g, w_dkv_pad, g_latent, w_ukv, ca, cb, n_heads, *, tm):
    m, d = h.shape
    rank = g_latent.shape[0]
    tm = _tile(m, tm)
    row = lambda i: (i, 0)
    fixed = lambda i: (0, 0)
    return pl.pallas_call(
        functools.partial(_latent_kv_kernel, n_heads=n_heads, rank=rank),
        out_shape=(jax.ShapeDtypeStruct((m, n_heads * MLA_QK_PAD), BF16),
                   jax.ShapeDtypeStruct((m, n_heads * MLA_V_DIM), BF16)),
        grid=(m // tm,),
        in_specs=[
            pl.BlockSpec((tm, d), row),
            pl.BlockSpec((1, d), fixed),
            pl.BlockSpec(w_dkv_pad.shape, fixed),
            pl.BlockSpec((1, rank), fixed),
            pl.BlockSpec(w_ukv.shape, fixed),
            pl.BlockSpec((tm, LANES), row),
            pl.BlockSpec((tm, LANES), row),
        ],
        out_specs=(pl.BlockSpec((tm, n_heads * MLA_QK_PAD), row),
                   pl.BlockSpec((tm, n_heads * MLA_V_DIM), row)),
        compiler_params=_params("parallel"),
        name="latent_kv",
    )(h, g.reshape(1, d), w_dkv_pad, g_latent.reshape(1, rank), w_ukv, ca, cb)


def _mla_in_kernel(h_ref, g_ref, win_ref, gq_ref, wuq_ref, ca_ref, cb_ref,
                   qcat_ref, memq_ref, *, n_heads, rank):
    rows = h_ref.shape[0] // ROW_SPLITS
    for c in range(ROW_SPLITS):
        r = slice(c * rows, (c + 1) * rows)
        xn = _rms(h_ref[r, :], g_ref[...]).astype(BF16)
        proj = jnp.dot(xn, win_ref[...], preferred_element_type=F32)
        memq_ref[r, :] = proj[:, rank:].astype(BF16)
        c_q = _rms(proj[:, :rank], gq_ref[...]).astype(BF16)
        ca, cb = ca_ref[r, :], cb_ref[r, :]
        for h in range(n_heads):
            q = jnp.dot(c_q, wuq_ref[:, h * MLA_QK_PAD:(h + 1) * MLA_QK_PAD], preferred_element_type=F32)
            qcat_ref[r, h * MLA_QK_PAD:h * MLA_QK_PAD + MLA_NOPE_DIM] = q[:, :MLA_NOPE_DIM].astype(BF16)
            qcat_ref[r, h * MLA_QK_PAD + MLA_NOPE_DIM:(h + 1) * MLA_QK_PAD] = (
                _rope(q[:, MLA_NOPE_DIM:], ca, cb).astype(BF16))


def _mla_in(h, g, w_in, g_q, w_uq_pad, ca, cb, n_heads, *, tm):
    m, d = h.shape
    rank = g_q.shape[0]
    n_memq = w_in.shape[1] - rank
    tm = _tile(m, tm)
    row = lambda i: (i, 0)
    fixed = lambda i: (0, 0)
    return pl.pallas_call(
        functools.partial(_mla_in_kernel, n_heads=n_heads, rank=rank),
        out_shape=(jax.ShapeDtypeStruct((m, n_heads * MLA_QK_PAD), BF16),
                   jax.ShapeDtypeStruct((m, n_memq), BF16)),
        grid=(m // tm,),
        in_specs=[
            pl.BlockSpec((tm, d), row),
            pl.BlockSpec((1, d), fixed),
            pl.BlockSpec(w_in.shape, fixed),
            pl.BlockSpec((1, rank), fixed),
            pl.BlockSpec(w_uq_pad.shape, fixed),
            pl.BlockSpec((tm, LANES), row),
            pl.BlockSpec((tm, LANES), row),
        ],
        out_specs=(pl.BlockSpec((tm, n_heads * MLA_QK_PAD), row),
                   pl.BlockSpec((tm, n_memq), row)),
        compiler_params=_params("parallel"),
        name="mla_in",
    )(h, g.reshape(1, d), w_in, g_q.reshape(1, rank), w_uq_pad, ca, cb)


def _rope_tables(positions):
    half = MLA_ROPE_DIM // 2
    inv_freq = ROPE_THETA ** (-jnp.arange(half, dtype=F32) / half)
    ang = positions.astype(F32)[..., None] * inv_freq
    cos, sin = jnp.cos(ang), jnp.sin(ang)
    zeros = jnp.zeros(cos.shape[:-1] + (LANES - MLA_ROPE_DIM,), F32)
    ca = jnp.concatenate([cos, cos, zeros], axis=-1)
    cb = jnp.concatenate([-sin, sin, zeros], axis=-1)
    return ca.reshape(-1, LANES), cb.reshape(-1, LANES)


def kernel(x, mem, positions, attn_norm_g, ffn_norm_g, a_w_in, a_w_out, b_w_in, b_q_norm_g, b_w_uq, b_w_out, mem_norm_g, w_mem_kv, kv_norm_g, w_dkv, kv_latent_g, w_ukv, ffn_w_gu, ffn_w_down, final_norm_g):
    b, s, d = x.shape
    mem_len = mem.shape[1]
    depth = attn_norm_g.shape[0]
    n_a = a_w_in.shape[0]
    n_b = b_w_in.shape[0]
    mq_w = MEM_HEADS * HEAD_DIM
    sb_heads = (a_w_in.shape[2] - mq_w) // (3 * HEAD_DIM)
    q_rank = b_q_norm_g.shape[1]
    kv_rank = kv_latent_g.shape[0]
    mla_heads = b_w_uq.shape[2] // (MLA_NOPE_DIM + MLA_ROPE_DIM)

    a_w_in_b = a_w_in.astype(BF16)
    a_w_out_b = a_w_out.astype(BF16)
    b_w_in_b = b_w_in.astype(BF16)
    b_w_out_b = b_w_out.astype(BF16)
    w_gu_b = ffn_w_gu.astype(BF16)
    w_d_b = ffn_w_down.astype(BF16)
    w_ukv_b = w_ukv.astype(BF16)
    w_mem_all = jnp.transpose(w_mem_kv, (1, 0, 2)).reshape(d, depth * 2 * mq_w).astype(BF16)
    w_dkv_pad = jnp.pad(w_dkv, ((0, 0), (0, LANES - MLA_ROPE_DIM))).astype(BF16)
    w_uq_pad = jnp.pad(
        b_w_uq.reshape(n_b, q_rank, mla_heads, MLA_NOPE_DIM + MLA_ROPE_DIM),
        ((0, 0), (0, 0), (0, 0), (0, MLA_QK_PAD - MLA_NOPE_DIM - MLA_ROPE_DIM)),
    ).reshape(n_b, q_rank, mla_heads * MLA_QK_PAD).astype(BF16)
    ca, cb = _rope_tables(positions)

    h = x.reshape(b * s, d)
    mkv = _norm_matmul(mem.reshape(b * mem_len, d), mem_norm_g, w_mem_all, tm=1024, tn=1024)
    mkv = mkv.reshape(b, mem_len, depth * 2 * mq_w)
    kv_blocks = 2 * MEM_HEADS

    k_cat = v_lat = None
    for layer in range(depth):
        if layer == n_a:
            k_cat, v_lat = _latent_kv(h, kv_norm_g, w_dkv_pad, kv_latent_g, w_ukv_b, ca, cb,
                                      mla_heads, tm=1024)
            k_cat = k_cat.reshape(b, s, -1)
            v_lat = v_lat.reshape(b, s, -1)
        if layer < n_a:
            proj = _norm_matmul(h, attn_norm_g[layer], a_w_in_b[layer], tm=1024, tn=1024)
            proj = proj.reshape(b, s, -1)
            mix = _sb_attention(proj, sb_heads, blk=256)
            memo = _mem_attention(proj, 3 * sb_heads, mkv, layer * kv_blocks, tq=2048)
            w_out = a_w_out_b[layer]
        else:
            li = layer - n_a
            q_cat, mem_q = _mla_in(h, attn_norm_g[layer], b_w_in_b[li], b_q_norm_g[li], w_uq_pad[li],
                                   ca, cb, mla_heads, tm=1024)
            mix = _mla_attention(q_cat.reshape(b, s, -1), k_cat, v_lat, mla_heads, blk=256)
            memo = _mem_attention(mem_q.reshape(b, s, -1), 0, mkv, layer * kv_blocks, tq=2048)
            w_out = b_w_out_b[li]
        h = _out_proj(mix.reshape(b * s, -1), memo.reshape(b * s, -1), w_out, h, tm=512)
        h = _ffn(h, ffn_norm_g[layer], w_gu_b[layer], w_d_b[layer], final_norm_g,
                 final_norm=(layer == depth - 1), tm=1024, tf=512)
    return h.reshape(b, s, d)
```

```python
import functools

import jax
import jax.numpy as jnp
from jax import lax
from jax.experimental import pallas as pl
from jax.experimental.pallas import tpu as pltpu

HEAD_DIM = 128
MEM_HEADS = 4
MLA_NOPE_DIM = 128
MLA_ROPE_DIM = 64
MLA_V_DIM = 128
ROPE_THETA = 10000.0
RMS_EPS = 1e-6
LOG2_E = 1.4426950408889634
SB_Q_PRESCALE = -(HEAD_DIM ** -0.5) * LOG2_E

LANES = 128
MLA_QK_PAD = 2 * LANES
VMEM_LIMIT_BYTES = 48 * 1024 * 1024
NORM_CHUNK_ROWS = 256
FFN_FIRST_STEP_SPLITS = 2
NORM_MATMUL_FIRST_STEP_SPLITS = 2
ROW_SPLITS = 2

BF16 = jnp.bfloat16
F32 = jnp.float32


VMEM_LIMIT_BYTES_FFN = 58 * 1024 * 1024


def _params(*semantics, vmem_limit_bytes=VMEM_LIMIT_BYTES):
    return pltpu.CompilerParams(dimension_semantics=semantics, vmem_limit_bytes=vmem_limit_bytes)


def _tile(n, pref):
    t = min(n, pref)
    assert n % t == 0, (n, pref)
    return t


def _rms(x, g):
    return (x * lax.rsqrt(jnp.mean(x * x, axis=-1, keepdims=True) + RMS_EPS)) * g


def _rmsnorm_rows(x_ref, g_ref, out_ref):
    rows = x_ref.shape[0]
    chunk = min(rows, NORM_CHUNK_ROWS)
    g = g_ref[...]

    def body(c, carry):
        r = pl.multiple_of(c * chunk, chunk)
        out_ref[pl.ds(r, chunk), :] = _rms(x_ref[pl.ds(r, chunk), :], g).astype(out_ref.dtype)
        return carry

    lax.fori_loop(0, rows // chunk, body, 0)


def _rope(x, ca, cb):
    partner = pltpu.roll(x, 96, 1) + pltpu.roll(x, 32, 1)
    return x * ca + partner * cb


def _norm_matmul_kernel(h_ref, g_ref, w_ref, cs_ref, o_ref, xn_ref):
    j = pl.program_id(1)

    @pl.when(j == 0)
    def _():
        rows = h_ref.shape[0] // NORM_MATMUL_FIRST_STEP_SPLITS
        g = g_ref[...]
        for c in range(NORM_MATMUL_FIRST_STEP_SPLITS):
            xn = _rms(h_ref[c * rows:(c + 1) * rows, :], g).astype(BF16)
            xn_ref[c * rows:(c + 1) * rows, :] = xn
            o_ref[c * rows:(c + 1) * rows, :] = (jnp.dot(
                xn, w_ref[...], preferred_element_type=F32) * cs_ref[...]).astype(o_ref.dtype)

    @pl.when(j > 0)
    def _():
        o_ref[...] = (jnp.dot(xn_ref[...], w_ref[...], preferred_element_type=F32)
                      * cs_ref[...]).astype(o_ref.dtype)


def _norm_matmul(h, g, w, layer, col_scale, *, tm, tn):
    m, d = h.shape
    n = w.shape[2]
    tm, tn = _tile(m, tm), _tile(n, tn)
    return pl.pallas_call(
        _norm_matmul_kernel,
        out_shape=jax.ShapeDtypeStruct((m, n), BF16),
        grid=(m // tm, n // tn),
        in_specs=[
            pl.BlockSpec((tm, d), lambda i, j: (i, 0)),
            pl.BlockSpec((1, d), lambda i, j: (0, 0)),
            pl.BlockSpec((None, d, tn), lambda i, j: (layer, 0, j)),
            pl.BlockSpec((1, tn), lambda i, j: (0, j)),
        ],
        out_specs=pl.BlockSpec((tm, tn), lambda i, j: (i, j)),
        scratch_shapes=[pltpu.VMEM((tm, d), BF16)],
        compiler_params=_params("parallel", "arbitrary"),
        name="norm_matmul",
    )(h, g.reshape(1, d), w, col_scale.reshape(1, n))


def _ffn_kernel(h_ref, g_ref, wg_ref, wu_ref, wd_ref, gf_ref, o_ref, xn_ref, *, final_norm):
    j = pl.program_id(1)

    def hidden_tile(xn):
        gate = jnp.dot(xn, wg_ref[...], preferred_element_type=F32)
        up = jnp.dot(xn, wu_ref[...], preferred_element_type=F32)
        act = (gate * jax.nn.sigmoid(gate) * up).astype(BF16)
        return jnp.dot(act, wd_ref[...], preferred_element_type=F32)

    @pl.when(j == 0)
    def _():
        half = h_ref.shape[0] // FFN_FIRST_STEP_SPLITS
        g = g_ref[...]
        for c in range(FFN_FIRST_STEP_SPLITS):
            h = h_ref[c * half:(c + 1) * half, :]
            xn = _rms(h, g).astype(BF16)
            xn_ref[c * half:(c + 1) * half, :] = xn
            o_ref[c * half:(c + 1) * half, :] = h + hidden_tile(xn)

    @pl.when(j > 0)
    def _():
        o_ref[...] += hidden_tile(xn_ref[...])

    if final_norm:
        @pl.when(j == pl.num_programs(1) - 1)
        def _():
            _rmsnorm_rows(o_ref, gf_ref, o_ref)


def _ffn(h, g, w_gu, w_d, layer, gf, *, final_norm, tm, tf):
    m, d = h.shape
    f = w_d.shape[1]
    tm, tf = _tile(m, tm), _tile(f, tf)
    nf = f // tf
    return pl.pallas_call(
        functools.partial(_ffn_kernel, final_norm=final_norm),
        out_shape=jax.ShapeDtypeStruct((m, d), F32),
        grid=(m // tm, nf),
        in_specs=[
            pl.BlockSpec((tm, d), lambda i, j: (i, 0)),
            pl.BlockSpec((1, d), lambda i, j: (0, 0)),
            pl.BlockSpec((None, d, tf), lambda i, j: (layer, 0, j)),
            pl.BlockSpec((None, d, tf), lambda i, j: (layer, 0, nf + j)),
            pl.BlockSpec((None, tf, d), lambda i, j: (layer, j, 0)),
            pl.BlockSpec((1, d), lambda i, j: (0, 0)),
        ],
        out_specs=pl.BlockSpec((tm, d), lambda i, j: (i, 0)),
        scratch_shapes=[pltpu.VMEM((tm, d), BF16)],
        compiler_params=_params("parallel", "arbitrary", vmem_limit_bytes=VMEM_LIMIT_BYTES_FFN),
        name="ffn",
    )(h, g.reshape(1, d), w_gu, w_gu, w_d, gf.reshape(1, d))


def _mem_out_proj_kernel(mix_ref, qm_ref, mk_ref, mv_ref, w1_ref, w2_ref, h_ref, o_ref, *, scale):
    acc = h_ref[...] + jnp.dot(mix_ref[...], w1_ref[...], preferred_element_type=F32)
    heads = []
    for hd in range(MEM_HEADS):
        cols = slice(hd * HEAD_DIM, (hd + 1) * HEAD_DIM)
        s = lax.dot_general(qm_ref[:, cols], mk_ref[:, cols], (((1,), (1,)), ((), ())),
                            preferred_element_type=F32)
        m = jnp.max(s, axis=1, keepdims=True)
        p = jnp.exp2((s - m) * (scale * LOG2_E))
        l = jnp.sum(p, axis=1, keepdims=True)
        o = jnp.dot(p.astype(BF16), mv_ref[:, cols], preferred_element_type=F32)
        heads.append((o / l).astype(BF16))
    mem_out = jnp.concatenate(heads, axis=1)
    o_ref[...] = acc + jnp.dot(mem_out, w2_ref[...], preferred_element_type=F32)


def _mem_out_proj(mix, q_src, q_block, mkv, kv_block, w_out, layer, h, *, tm):
    m, d = h.shape
    bsz, ml, _ = mkv.shape
    k1, k2 = mix.shape[1], MEM_HEADS * HEAD_DIM
    assert k1 % k2 == 0 and w_out.shape[1] == k1 + k2
    tm = _tile(m // bsz, tm)
    steps_per_batch = (m // bsz) // tm
    return pl.pallas_call(
        functools.partial(_mem_out_proj_kernel, scale=HEAD_DIM ** -0.5),
        out_shape=jax.ShapeDtypeStruct((m, d), F32),
        grid=(m // tm,),
        in_specs=[
            pl.BlockSpec((tm, k1), lambda i: (i, 0)),
            pl.BlockSpec((tm, k2), lambda i: (i, q_block)),
            pl.BlockSpec((None, ml, k2), lambda i: (i // steps_per_batch, 0, kv_block)),
            pl.BlockSpec((None, ml, k2), lambda i: (i // steps_per_batch, 0, kv_block + 1)),
            pl.BlockSpec((None, k1, d), lambda i: (layer, 0, 0)),
            pl.BlockSpec((None, k2, d), lambda i: (layer, k1 // k2, 0)),
            pl.BlockSpec((tm, d), lambda i: (i, 0)),
        ],
        out_specs=pl.BlockSpec((tm, d), lambda i: (i, 0)),
        compiler_params=_params("parallel"),
        name="mem_out_proj",
    )(mix, q_src, mkv, mkv, w_out, w_out, h)


def _sb_attn_kernel(q_ref, k_ref, v_ref, tri_ref, o_ref, *, blk):
    tri2 = tri_ref[...]
    row = lax.broadcasted_iota(jnp.int32, (blk, blk), 0)
    col = lax.broadcasted_iota(jnp.int32, (blk, blk), 1)
    strictly_past = col < row
    nq = q_ref.shape[0] // blk
    heads = q_ref.shape[1] // HEAD_DIM

    def logits(i, hd):
        q = q_ref[i * blk:(i + 1) * blk, hd * HEAD_DIM:(hd + 1) * HEAD_DIM]
        k = k_ref[0:(i + 1) * blk, hd * HEAD_DIM:(hd + 1) * HEAD_DIM]
        return lax.dot_general(q, k, (((1,), (1,)), ((), ())), preferred_element_type=F32)

    def log_keep(nz):
        neg_abs = lax.bitcast_convert_type(
            lax.bitcast_convert_type(nz, jnp.uint32) | jnp.uint32(0x80000000), F32)
        return jnp.minimum(nz, 0.0) - jnp.log2(1.0 + jnp.exp2(neg_abs))

    def finish(i, hd, nz, lkeep):
        w_blocks = [None] * (i + 1)
        carry = None
        for j in range(i, -1, -1):
            nzj = nz[:, j * blk:(j + 1) * blk]
            lk = lkeep[:, j * blk:(j + 1) * blk]
            if j == i:
                lk = jnp.where(strictly_past, lk, 0.0)
            hi = lk.astype(BF16)
            lo = (lk - hi.astype(F32)).astype(BF16)
            log_w = jnp.dot(jnp.concatenate([hi, lo], axis=1), tri2, preferred_element_type=F32) - nzj
            if carry is not None:
                log_w = log_w + carry
            w = jnp.exp2(log_w)
            if j == i:
                w = jnp.where(strictly_past, w, 0.0)
            w_blocks[j] = w.astype(BF16)
            block_sum = jnp.sum(lk, axis=1, keepdims=True)
            carry = block_sum if carry is None else carry + block_sum
        w = w_blocks[0] if i == 0 else jnp.concatenate(w_blocks, axis=1)
        v = v_ref[0:(i + 1) * blk, hd * HEAD_DIM:(hd + 1) * HEAD_DIM]
        o_ref[i * blk:(i + 1) * blk, hd * HEAD_DIM:(hd + 1) * HEAD_DIM] = jnp.dot(
            w, v, preferred_element_type=F32).astype(o_ref.dtype)

    order = [(i, hd) for i in range(nq - 1, -1, -1) for hd in range(heads)]
    nz_of, lk_of = {}, {}
    for step in range(len(order) + 2):
        if step < len(order):
            nz_of[order[step]] = logits(*order[step])
        if 0 <= step - 1 < len(order):
            lk_of[order[step - 1]] = log_keep(nz_of[order[step - 1]])
        if 0 <= step - 2 < len(order):
            unit = order[step - 2]
            finish(*unit, nz_of.pop(unit), lk_of.pop(unit))


def _sb_attention(proj, n_heads, *, blk, heads_per_step):
    b, s, _ = proj.shape
    blk = _tile(s, blk)
    tri = (jnp.arange(blk)[:, None] >= jnp.arange(blk)[None, :]).astype(BF16)
    tri2 = jnp.concatenate([tri, tri], axis=0)
    assert n_heads % heads_per_step == 0
    groups, width = n_heads // heads_per_step, heads_per_step * HEAD_DIM
    return pl.pallas_call(
        functools.partial(_sb_attn_kernel, blk=blk),
        out_shape=jax.ShapeDtypeStruct((b, s, n_heads * HEAD_DIM), BF16),
        grid=(b, groups),
        in_specs=[
            pl.BlockSpec((None, s, width), lambda bi, h: (bi, 0, h)),
            pl.BlockSpec((None, s, width), lambda bi, h: (bi, 0, groups + h)),
            pl.BlockSpec((None, s, width), lambda bi, h: (bi, 0, 2 * groups + h)),
            pl.BlockSpec((2 * blk, blk), lambda bi, h: (0, 0)),
        ],
        out_specs=pl.BlockSpec((None, s, width), lambda bi, h: (bi, 0, h)),
        compiler_params=_params("parallel", "parallel"),
        name="sb_attention",
    )(proj, proj, proj, tri2)


def _mla_attn_kernel(q_ref, k_ref, v_ref, o_ref, *, blk, scale):
    row = lax.broadcasted_iota(jnp.int32, (blk, blk), 0)
    col = lax.broadcasted_iota(jnp.int32, (blk, blk), 1)
    causal = col <= row
    nq = q_ref.shape[0] // blk

    def scores(i):
        q = q_ref[i * blk:(i + 1) * blk, :]
        return lax.dot_general(q, k_ref[0:(i + 1) * blk, :], (((1,), (1,)), ((), ())),
                               preferred_element_type=F32)

    def probs(i, s):
        diag = jnp.where(causal, s[:, i * blk:], -jnp.inf)
        s = diag if i == 0 else jnp.concatenate([s[:, :i * blk], diag], axis=1)
        m = jnp.max(s, axis=1, keepdims=True)
        p = jnp.exp2((s - m) * (scale * LOG2_E))
        return p.astype(BF16), jnp.sum(p, axis=1, keepdims=True)

    def finish(i, p, l):
        o = jnp.dot(p, v_ref[0:(i + 1) * blk, :], preferred_element_type=F32)
        o_ref[i * blk:(i + 1) * blk, :] = (o / l).astype(o_ref.dtype)

    order = list(range(nq - 1, -1, -1))
    s_of, p_of = {}, {}
    for step in range(nq + 2):
        if step < nq:
            s_of[order[step]] = scores(order[step])
        if 0 <= step - 1 < nq:
            i = order[step - 1]
            p_of[i] = probs(i, s_of.pop(i))
        if 0 <= step - 2 < nq:
            i = order[step - 2]
            finish(i, *p_of.pop(i))


def _mla_attention(q_cat, k_cat, v, n_heads, *, blk):
    b, s, _ = q_cat.shape
    blk = _tile(s, blk)
    scale = (MLA_NOPE_DIM + MLA_ROPE_DIM) ** -0.5
    return pl.pallas_call(
        functools.partial(_mla_attn_kernel, blk=blk, scale=scale),
        out_shape=jax.ShapeDtypeStruct((b, s, n_heads * MLA_V_DIM), BF16),
        grid=(b, n_heads),
        in_specs=[
            pl.BlockSpec((None, s, MLA_QK_PAD), lambda bi, h: (bi, 0, h)),
            pl.BlockSpec((None, s, MLA_QK_PAD), lambda bi, h: (bi, 0, h)),
            pl.BlockSpec((None, s, MLA_V_DIM), lambda bi, h: (bi, 0, h)),
        ],
        out_specs=pl.BlockSpec((None, s, MLA_V_DIM), lambda bi, h: (bi, 0, h)),
        compiler_params=_params("parallel", "parallel"),
        name="mla_attention",
    )(q_cat, k_cat, v)


def _latent_kv_kernel(h_ref, g_ref, wdkv_ref, gl_ref, wukv_ref, ca_ref, cb_ref,
                      kcat_ref, v_ref, *, n_heads, rank):
    rows = h_ref.shape[0] // ROW_SPLITS
    for c in range(ROW_SPLITS):
        r = slice(c * rows, (c + 1) * rows)
        xn = _rms(h_ref[r, :], g_ref[...]).astype(BF16)
        ckv = jnp.dot(xn, wdkv_ref[...], preferred_element_type=F32)
        c_latent = _rms(ckv[:, :rank], gl_ref[...]).astype(BF16)
        k_rope = _rope(ckv[:, rank:], ca_ref[r, :], cb_ref[r, :]).astype(BF16)
        for h in range(n_heads):
            kv = jnp.dot(c_latent, wukv_ref[:, h * 256:(h + 1) * 256], preferred_element_type=F32)
            kcat_ref[r, h * MLA_QK_PAD:h * MLA_QK_PAD + MLA_NOPE_DIM] = kv[:, :MLA_NOPE_DIM].astype(BF16)
            kcat_ref[r, h * MLA_QK_PAD + MLA_NOPE_DIM:(h + 1) * MLA_QK_PAD] = k_rope
            v_ref[r, h * MLA_V_DIM:(h + 1) * MLA_V_DIM] = kv[:, MLA_NOPE_DIM:].astype(BF16)


def _latent_kv(h, g, w_dkv_pad, g_latent, w_ukv, ca, cb, n_heads, *, tm):
    m, d = h.shape
    rank = g_latent.shape[0]
    tm = _tile(m, tm)
    row = lambda i: (i, 0)
    fixed = lambda i: (0, 0)
    return pl.pallas_call(
        functools.partial(_latent_kv_kernel, n_heads=n_heads, rank=rank),
        out_shape=(jax.ShapeDtypeStruct((m, n_heads * MLA_QK_PAD), BF16),
                   jax.ShapeDtypeStruct((m, n_heads * MLA_V_DIM), BF16)),
        grid=(m // tm,),
        in_specs=[
            pl.BlockSpec((tm, d), row),
            pl.BlockSpec((1, d), fixed),
            pl.BlockSpec(w_dkv_pad.shape, fixed),
            pl.BlockSpec((1, rank), fixed),
            pl.BlockSpec(w_ukv.shape, fixed),
            pl.BlockSpec((tm, LANES), row),
            pl.BlockSpec((tm, LANES), row),
        ],
        out_specs=(pl.BlockSpec((tm, n_heads * MLA_QK_PAD), row),
                   pl.BlockSpec((tm, n_heads * MLA_V_DIM), row)),
        compiler_params=_params("parallel"),
        name="latent_kv",
    )(h, g.reshape(1, d), w_dkv_pad, g_latent.reshape(1, rank), w_ukv, ca, cb)


def _mla_in_kernel(h_ref, g_ref, win_ref, gq_ref, wuq_ref, ca_ref, cb_ref,
                   qcat_ref, memq_ref, *, n_heads, rank):
    rows = h_ref.shape[0] // ROW_SPLITS
    for c in range(ROW_SPLITS):
        r = slice(c * rows, (c + 1) * rows)
        xn = _rms(h_ref[r, :], g_ref[...]).astype(BF16)
        proj = jnp.dot(xn, win_ref[...], preferred_element_type=F32)
        memq_ref[r, :] = proj[:, rank:].astype(BF16)
        c_q = _rms(proj[:, :rank], gq_ref[...]).astype(BF16)
        ca, cb = ca_ref[r, :], cb_ref[r, :]
        for h in range(n_heads):
            q = jnp.dot(c_q, wuq_ref[:, h * MLA_QK_PAD:(h + 1) * MLA_QK_PAD], preferred_element_type=F32)
            qcat_ref[r, h * MLA_QK_PAD:h * MLA_QK_PAD + MLA_NOPE_DIM] = q[:, :MLA_NOPE_DIM].astype(BF16)
            qcat_ref[r, h * MLA_QK_PAD + MLA_NOPE_DIM:(h + 1) * MLA_QK_PAD] = (
                _rope(q[:, MLA_NOPE_DIM:], ca, cb).astype(BF16))


def _mla_in(h, g, w_in, g_q, w_uq_pad, layer, ca, cb, n_heads, *, tm):
    m, d = h.shape
    rank = g_q.shape[0]
    n_memq = w_in.shape[2] - rank
    tm = _tile(m, tm)
    row = lambda i: (i, 0)
    fixed = lambda i: (0, 0)
    stacked = lambda i: (layer, 0, 0)
    return pl.pallas_call(
        functools.partial(_mla_in_kernel, n_heads=n_heads, rank=rank),
        out_shape=(jax.ShapeDtypeStruct((m, n_heads * MLA_QK_PAD), BF16),
                   jax.ShapeDtypeStruct((m, n_memq), BF16)),
        grid=(m // tm,),
        in_specs=[
            pl.BlockSpec((tm, d), row),
            pl.BlockSpec((1, d), fixed),
            pl.BlockSpec((None,) + w_in.shape[1:], stacked),
            pl.BlockSpec((1, rank), fixed),
            pl.BlockSpec((None,) + w_uq_pad.shape[1:], stacked),
            pl.BlockSpec((tm, LANES), row),
            pl.BlockSpec((tm, LANES), row),
        ],
        out_specs=(pl.BlockSpec((tm, n_heads * MLA_QK_PAD), row),
                   pl.BlockSpec((tm, n_memq), row)),
        compiler_params=_params("parallel"),
        name="mla_in",
    )(h, g.reshape(1, d), w_in, g_q.reshape(1, rank), w_uq_pad, ca, cb)


def _rope_tables(positions):
    half = MLA_ROPE_DIM // 2
    inv_freq = ROPE_THETA ** (-jnp.arange(half, dtype=F32) / half)
    ang = positions.astype(F32)[..., None] * inv_freq
    cos, sin = jnp.cos(ang), jnp.sin(ang)
    zeros = jnp.zeros(cos.shape[:-1] + (LANES - MLA_ROPE_DIM,), F32)
    ca = jnp.concatenate([cos, cos, zeros], axis=-1)
    cb = jnp.concatenate([-sin, sin, zeros], axis=-1)
    return ca.reshape(-1, LANES), cb.reshape(-1, LANES)


def kernel(x, mem, positions, attn_norm_g, ffn_norm_g, a_w_in, a_w_out, b_w_in, b_q_norm_g, b_w_uq, b_w_out, mem_norm_g, w_mem_kv, kv_norm_g, w_dkv, kv_latent_g, w_ukv, ffn_w_gu, ffn_w_down, final_norm_g):
    b, s, d = x.shape
    mem_len = mem.shape[1]
    depth = attn_norm_g.shape[0]
    n_a = a_w_in.shape[0]
    n_b = b_w_in.shape[0]
    mq_w = MEM_HEADS * HEAD_DIM
    sb_heads = (a_w_in.shape[2] - mq_w) // (3 * HEAD_DIM)
    q_rank = b_q_norm_g.shape[1]
    kv_rank = kv_latent_g.shape[0]
    mla_heads = b_w_uq.shape[2] // (MLA_NOPE_DIM + MLA_ROPE_DIM)

    a_w_in_b = a_w_in.astype(BF16)
    a_w_out_b = a_w_out.astype(BF16)
    b_w_in_b = b_w_in.astype(BF16)
    b_w_out_b = b_w_out.astype(BF16)
    w_gu_b = ffn_w_gu.astype(BF16)
    w_d_b = ffn_w_down.astype(BF16)
    w_ukv_b = w_ukv.astype(BF16)
    w_mem_all = jnp.transpose(w_mem_kv, (1, 0, 2)).reshape(d, depth * 2 * mq_w).astype(BF16)
    w_dkv_pad = jnp.pad(w_dkv, ((0, 0), (0, LANES - MLA_ROPE_DIM))).astype(BF16)
    w_uq_pad = jnp.pad(
        b_w_uq.reshape(n_b, q_rank, mla_heads, MLA_NOPE_DIM + MLA_ROPE_DIM),
        ((0, 0), (0, 0), (0, 0), (0, MLA_QK_PAD - MLA_NOPE_DIM - MLA_ROPE_DIM)),
    ).reshape(n_b, q_rank, mla_heads * MLA_QK_PAD).astype(BF16)
    ca, cb = _rope_tables(positions)
    a_col_scale = jnp.concatenate([jnp.full((sb_heads * HEAD_DIM,), SB_Q_PRESCALE, F32),
                                   jnp.ones((a_w_in.shape[2] - sb_heads * HEAD_DIM,), F32)])

    h = x.reshape(b * s, d)
    mkv = _norm_matmul(mem.reshape(b * mem_len, d), mem_norm_g, w_mem_all[None], 0,
                       jnp.ones((w_mem_all.shape[1],), F32), tm=1024, tn=1024)
    mkv = mkv.reshape(b, mem_len, depth * 2 * mq_w)

    k_cat = v_lat = None
    for layer in range(depth):
        if layer == n_a:
            k_cat, v_lat = _latent_kv(h, kv_norm_g, w_dkv_pad, kv_latent_g, w_ukv_b, ca, cb,
                                      mla_heads, tm=1024)
            k_cat = k_cat.reshape(b, s, -1)
            v_lat = v_lat.reshape(b, s, -1)
        if layer < n_a:
            proj = _norm_matmul(h, attn_norm_g[layer], a_w_in_b, layer, a_col_scale, tm=1024, tn=1024)
            mix = _sb_attention(proj.reshape(b, s, -1), sb_heads, blk=256, heads_per_step=1)
            q_src, q_block = proj, (3 * sb_heads * HEAD_DIM) // mq_w
            w_out, li = a_w_out_b, layer
        else:
            li = layer - n_a
            q_cat, mem_q = _mla_in(h, attn_norm_g[layer], b_w_in_b, b_q_norm_g[li], w_uq_pad, li,
                                   ca, cb, mla_heads, tm=1024)
            mix = _mla_attention(q_cat.reshape(b, s, -1), k_cat, v_lat, mla_heads, blk=256)
            q_src, q_block = mem_q, 0
            w_out = b_w_out_b
        h = _mem_out_proj(mix.reshape(b * s, -1), q_src, q_block, mkv, 2 * layer, w_out, li, h, tm=512)
        h = _ffn(h, ffn_norm_g[layer], w_gu_b, w_d_b, layer, final_norm_g,
                 final_norm=(layer == depth - 1), tm=1024, tf=512)
    return h.reshape(b, s, d)
```

```python
import functools

import jax
import jax.numpy as jnp
from jax import lax
from jax.experimental import pallas as pl
from jax.experimental.pallas import tpu as pltpu

HEAD_DIM = 128
MEM_HEADS = 4
MLA_NOPE_DIM = 128
MLA_ROPE_DIM = 64
MLA_V_DIM = 128
ROPE_THETA = 10000.0
RMS_EPS = 1e-6
LOG2_E = 1.4426950408889634
SB_Q_PRESCALE = -(HEAD_DIM ** -0.5) * LOG2_E
SB_NEAR_KEY_BLOCKS = 2
SB_ZERO_WEIGHT_LOG2 = -140.0

LANES = 128
MLA_QK_PAD = 2 * LANES
VMEM_LIMIT_BYTES = 48 * 1024 * 1024
NORM_CHUNK_ROWS = 256
FFN_FIRST_STEP_SPLITS = 2
NORM_MATMUL_FIRST_STEP_SPLITS = 2
ROW_SPLITS = 2

BF16 = jnp.bfloat16
F32 = jnp.float32


VMEM_LIMIT_BYTES_FFN = 58 * 1024 * 1024


def _params(*semantics, vmem_limit_bytes=VMEM_LIMIT_BYTES):
    return pltpu.CompilerParams(dimension_semantics=semantics, vmem_limit_bytes=vmem_limit_bytes)


def _tile(n, pref):
    t = min(n, pref)
    assert n % t == 0, (n, pref)
    return t


def _rms(x, g):
    return (x * lax.rsqrt(jnp.mean(x * x, axis=-1, keepdims=True) + RMS_EPS)) * g


def _rmsnorm_rows(x_ref, g_ref, out_ref):
    rows = x_ref.shape[0]
    chunk = min(rows, NORM_CHUNK_ROWS)
    g = g_ref[...]

    def body(c, carry):
        r = pl.multiple_of(c * chunk, chunk)
        out_ref[pl.ds(r, chunk), :] = _rms(x_ref[pl.ds(r, chunk), :], g).astype(out_ref.dtype)
        return carry

    lax.fori_loop(0, rows // chunk, body, 0)


def _rope(x, ca, cb):
    partner = pltpu.roll(x, 96, 1) + pltpu.roll(x, 32, 1)
    return x * ca + partner * cb


def _norm_matmul_kernel(h_ref, g_ref, w_ref, cs_ref, o_ref, xn_ref):
    j = pl.program_id(1)

    @pl.when(j == 0)
    def _():
        rows = h_ref.shape[0] // NORM_MATMUL_FIRST_STEP_SPLITS
        g = g_ref[...]
        for c in range(NORM_MATMUL_FIRST_STEP_SPLITS):
            xn = _rms(h_ref[c * rows:(c + 1) * rows, :], g).astype(BF16)
            xn_ref[c * rows:(c + 1) * rows, :] = xn
            o_ref[c * rows:(c + 1) * rows, :] = (jnp.dot(
                xn, w_ref[...], preferred_element_type=F32) * cs_ref[...]).astype(o_ref.dtype)

    @pl.when(j > 0)
    def _():
        o_ref[...] = (jnp.dot(xn_ref[...], w_ref[...], preferred_element_type=F32)
                      * cs_ref[...]).astype(o_ref.dtype)


def _norm_matmul(h, g, w, layer, col_scale, *, tm, tn):
    m, d = h.shape
    n = w.shape[2]
    tm, tn = _tile(m, tm), _tile(n, tn)
    return pl.pallas_call(
        _norm_matmul_kernel,
        out_shape=jax.ShapeDtypeStruct((m, n), BF16),
        grid=(m // tm, n // tn),
        in_specs=[
            pl.BlockSpec((tm, d), lambda i, j: (i, 0)),
            pl.BlockSpec((1, d), lambda i, j: (0, 0)),
            pl.BlockSpec((None, d, tn), lambda i, j: (layer, 0, j)),
            pl.BlockSpec((1, tn), lambda i, j: (0, j)),
        ],
        out_specs=pl.BlockSpec((tm, tn), lambda i, j: (i, j)),
        scratch_shapes=[pltpu.VMEM((tm, d), BF16)],
        compiler_params=_params("parallel", "arbitrary"),
        name="norm_matmul",
    )(h, g.reshape(1, d), w, col_scale.reshape(1, n))


def _ffn_kernel(h_ref, g_ref, wg_ref, wu_ref, wd_ref, gf_ref, o_ref, xn_ref, *, final_norm):
    j = pl.program_id(1)

    def hidden_tile(xn):
        gate = jnp.dot(xn, wg_ref[...], preferred_element_type=F32)
        up = jnp.dot(xn, wu_ref[...], preferred_element_type=F32)
        act = (gate * jax.nn.sigmoid(gate) * up).astype(BF16)
        return jnp.dot(act, wd_ref[...], preferred_element_type=F32)

    @pl.when(j == 0)
    def _():
        half = h_ref.shape[0] // FFN_FIRST_STEP_SPLITS
        g = g_ref[...]
        for c in range(FFN_FIRST_STEP_SPLITS):
            h = h_ref[c * half:(c + 1) * half, :]
            xn = _rms(h, g).astype(BF16)
            xn_ref[c * half:(c + 1) * half, :] = xn
            o_ref[c * half:(c + 1) * half, :] = h + hidden_tile(xn)

    @pl.when(j > 0)
    def _():
        o_ref[...] += hidden_tile(xn_ref[...])

    if final_norm:
        @pl.when(j == pl.num_programs(1) - 1)
        def _():
            _rmsnorm_rows(o_ref, gf_ref, o_ref)


def _ffn(h, g, w_gu, w_d, layer, gf, *, final_norm, tm, tf):
    m, d = h.shape
    f = w_d.shape[1]
    tm, tf = _tile(m, tm), _tile(f, tf)
    nf = f // tf
    return pl.pallas_call(
        functools.partial(_ffn_kernel, final_norm=final_norm),
        out_shape=jax.ShapeDtypeStruct((m, d), F32),
        grid=(m // tm, nf),
        in_specs=[
            pl.BlockSpec((tm, d), lambda i, j: (i, 0)),
            pl.BlockSpec((1, d), lambda i, j: (0, 0)),
            pl.BlockSpec((None, d, tf), lambda i, j: (layer, 0, j)),
            pl.BlockSpec((None, d, tf), lambda i, j: (layer, 0, nf + j)),
            pl.BlockSpec((None, tf, d), lambda i, j: (layer, j, 0)),
            pl.BlockSpec((1, d), lambda i, j: (0, 0)),
        ],
        out_specs=pl.BlockSpec((tm, d), lambda i, j: (i, 0)),
        scratch_shapes=[pltpu.VMEM((tm, d), BF16)],
        compiler_params=_params("parallel", "arbitrary", vmem_limit_bytes=VMEM_LIMIT_BYTES_FFN),
        name="ffn",
    )(h, g.reshape(1, d), w_gu, w_gu, w_d, gf.reshape(1, d))


def _mem_out_proj_kernel(mix_ref, qm_ref, mk_ref, mv_ref, w1_ref, w2_ref, h_ref, o_ref, *, scale):
    acc = h_ref[...] + jnp.dot(mix_ref[...], w1_ref[...], preferred_element_type=F32)
    heads = []
    for hd in range(MEM_HEADS):
        cols = slice(hd * HEAD_DIM, (hd + 1) * HEAD_DIM)
        s = lax.dot_general(qm_ref[:, cols], mk_ref[:, cols], (((1,), (1,)), ((), ())),
                            preferred_element_type=F32)
        m = jnp.max(s, axis=1, keepdims=True)
        p = jnp.exp2((s - m) * (scale * LOG2_E))
        l = jnp.sum(p, axis=1, keepdims=True)
        o = jnp.dot(p.astype(BF16), mv_ref[:, cols], preferred_element_type=F32)
        heads.append((o / l).astype(BF16))
    mem_out = jnp.concatenate(heads, axis=1)
    o_ref[...] = acc + jnp.dot(mem_out, w2_ref[...], preferred_element_type=F32)


def _mem_out_proj(mix, q_src, q_block, mkv, kv_block, w_out, layer, h, *, tm):
    m, d = h.shape
    bsz, ml, _ = mkv.shape
    k1, k2 = mix.shape[1], MEM_HEADS * HEAD_DIM
    assert k1 % k2 == 0 and w_out.shape[1] == k1 + k2
    tm = _tile(m // bsz, tm)
    steps_per_batch = (m // bsz) // tm
    return pl.pallas_call(
        functools.partial(_mem_out_proj_kernel, scale=HEAD_DIM ** -0.5),
        out_shape=jax.ShapeDtypeStruct((m, d), F32),
        grid=(m // tm,),
        in_specs=[
            pl.BlockSpec((tm, k1), lambda i: (i, 0)),
            pl.BlockSpec((tm, k2), lambda i: (i, q_block)),
            pl.BlockSpec((None, ml, k2), lambda i: (i // steps_per_batch, 0, kv_block)),
            pl.BlockSpec((None, ml, k2), lambda i: (i // steps_per_batch, 0, kv_block + 1)),
            pl.BlockSpec((None, k1, d), lambda i: (layer, 0, 0)),
            pl.BlockSpec((None, k2, d), lambda i: (layer, k1 // k2, 0)),
            pl.BlockSpec((tm, d), lambda i: (i, 0)),
        ],
        out_specs=pl.BlockSpec((tm, d), lambda i: (i, 0)),
        compiler_params=_params("parallel"),
        name="mem_out_proj",
    )(mix, q_src, mkv, mkv, w_out, w_out, h)


def _sb_attn_kernel(q_ref, k_ref, v_ref, tri_ref, o_ref, acc_ref, carry_ref, *, blk):
    tri2 = tri_ref[...]
    row = lax.broadcasted_iota(jnp.int32, (blk, blk), 0)
    col = lax.broadcasted_iota(jnp.int32, (blk, blk), 1)
    strictly_past = col < row
    nq = q_ref.shape[0] // blk

    def rows(i):
        return slice(i * blk, (i + 1) * blk)

    def log_keep(nz):
        neg_abs = lax.bitcast_convert_type(
            lax.bitcast_convert_type(nz, jnp.uint32) | jnp.uint32(0x80000000), F32)
        return jnp.minimum(nz, 0.0) - jnp.log2(1.0 + jnp.exp2(neg_abs))

    def block_weights(nz, lk, carry, diagonal):
        if diagonal:
            lk = jnp.where(strictly_past, lk, 0.0)
        hi = lk.astype(BF16)
        lo = (lk - hi.astype(F32)).astype(BF16)
        log_w = jnp.dot(jnp.concatenate([hi, lo], axis=1), tri2, preferred_element_type=F32) - nz
        if carry is not None:
            log_w = log_w + carry
        w = jnp.exp2(log_w)
        if diagonal:
            w = jnp.where(strictly_past, w, 0.0)
        block_sum = jnp.sum(lk, axis=1, keepdims=True)
        return w.astype(BF16), (block_sum if carry is None else carry + block_sum)

    def first_key_block(i):
        return max(i - SB_NEAR_KEY_BLOCKS + 1, 0)

    def logits(i):
        keys = slice(first_key_block(i) * blk, (i + 1) * blk)
        return lax.dot_general(q_ref[rows(i), :], k_ref[keys, :], (((1,), (1,)), ((), ())),
                               preferred_element_type=F32)

    def finish(i, nz, lkeep):
        j0 = first_key_block(i)
        w_blocks, carry = [None] * (i + 1 - j0), None
        for j in range(i, j0 - 1, -1):
            cols = slice((j - j0) * blk, (j - j0 + 1) * blk)
            w_blocks[j - j0], carry = block_weights(nz[:, cols], lkeep[:, cols], carry, j == i)
        w = w_blocks[0] if len(w_blocks) == 1 else jnp.concatenate(w_blocks, axis=1)
        acc = jnp.dot(w, v_ref[j0 * blk:(i + 1) * blk, :], preferred_element_type=F32)
        o_ref[rows(i), :] = acc.astype(o_ref.dtype)
        if j0 > 0:
            acc_ref[rows(i), :] = acc
            carry_ref[rows(i), :] = jnp.broadcast_to(carry, (blk, LANES))
            q32 = q_ref[rows(i), :].astype(F32)
            logit_bound[i] = jnp.sqrt(jnp.max(jnp.sum(q32 * q32, axis=1, keepdims=True)) * k_sq) * 1.01 + 1.0
            carry_max[i] = jnp.max(carry)

    if nq > SB_NEAR_KEY_BLOCKS:
        k32 = k_ref[...].astype(F32)
        k_sq = jnp.max(jnp.sum(k32 * k32, axis=1, keepdims=True))
    logit_bound, carry_max = {}, {}

    order = list(range(nq - 1, -1, -1))
    nz_of, lk_of = {}, {}
    for step in range(nq + 2):
        if step < nq:
            nz_of[order[step]] = logits(order[step])
        if 0 <= step - 1 < nq:
            lk_of[order[step - 1]] = log_keep(nz_of[order[step - 1]])
        if 0 <= step - 2 < nq:
            i = order[step - 2]
            finish(i, nz_of.pop(i), lk_of.pop(i))

    for i in range(SB_NEAR_KEY_BLOCKS, nq):
        def more(state, i=i):
            j, c_max = state
            return jnp.logical_and(j >= 0, c_max + logit_bound[i] >= SB_ZERO_WEIGHT_LOG2)

        def take_block(state, i=i):
            j, _ = state
            keys = pl.ds(pl.multiple_of(j * blk, blk), blk)
            nz = lax.dot_general(q_ref[rows(i), :], k_ref[keys, :], (((1,), (1,)), ((), ())),
                                 preferred_element_type=F32)
            w, carry = block_weights(nz, log_keep(nz), carry_ref[rows(i), 0:1], False)
            acc_ref[rows(i), :] += jnp.dot(w, v_ref[keys, :], preferred_element_type=F32)
            carry_ref[rows(i), :] = jnp.broadcast_to(carry, (blk, LANES))
            return j - 1, jnp.max(carry)

        j_end, _ = lax.while_loop(more, take_block, (jnp.int32(i - SB_NEAR_KEY_BLOCKS), carry_max[i]))

        @pl.when(j_end < i - SB_NEAR_KEY_BLOCKS)
        def _(i=i):
            o_ref[rows(i), :] = acc_ref[rows(i), :].astype(o_ref.dtype)


def _sb_attention(proj, n_heads, *, blk):
    b, s, _ = proj.shape
    blk = _tile(s, blk)
    tri = (jnp.arange(blk)[:, None] >= jnp.arange(blk)[None, :]).astype(BF16)
    tri2 = jnp.concatenate([tri, tri], axis=0)
    return pl.pallas_call(
        functools.partial(_sb_attn_kernel, blk=blk),
        out_shape=jax.ShapeDtypeStruct((b, s, n_heads * HEAD_DIM), BF16),
        grid=(b, n_heads),
        in_specs=[
            pl.BlockSpec((None, s, HEAD_DIM), lambda bi, h: (bi, 0, h)),
            pl.BlockSpec((None, s, HEAD_DIM), lambda bi, h: (bi, 0, n_heads + h)),
            pl.BlockSpec((None, s, HEAD_DIM), lambda bi, h: (bi, 0, 2 * n_heads + h)),
            pl.BlockSpec((2 * blk, blk), lambda bi, h: (0, 0)),
        ],
        out_specs=pl.BlockSpec((None, s, HEAD_DIM), lambda bi, h: (bi, 0, h)),
        scratch_shapes=[pltpu.VMEM((s, HEAD_DIM), F32), pltpu.VMEM((s, LANES), F32)],
        compiler_params=_params("parallel", "parallel"),
        name="sb_attention",
    )(proj, proj, proj, tri2)


def _mla_attn_kernel(q_ref, k_ref, v_ref, o_ref, *, blk, scale):
    row = lax.broadcasted_iota(jnp.int32, (blk, blk), 0)
    col = lax.broadcasted_iota(jnp.int32, (blk, blk), 1)
    causal = col <= row
    nq = q_ref.shape[0] // blk

    def scores(i):
        q = q_ref[i * blk:(i + 1) * blk, :]
        return lax.dot_general(q, k_ref[0:(i + 1) * blk, :], (((1,), (1,)), ((), ())),
                               preferred_element_type=F32)

    def probs(i, s):
        diag = jnp.where(causal, s[:, i * blk:], -jnp.inf)
        s = diag if i == 0 else jnp.concatenate([s[:, :i * blk], diag], axis=1)
        m = jnp.max(s, axis=1, keepdims=True)
        p = jnp.exp2((s - m) * (scale * LOG2_E))
        return p.astype(BF16), jnp.sum(p, axis=1, keepdims=True)

    def finish(i, p, l):
        o = jnp.dot(p, v_ref[0:(i + 1) * blk, :], preferred_element_type=F32)
        o_ref[i * blk:(i + 1) * blk, :] = (o / l).astype(o_ref.dtype)

    order = list(range(nq - 1, -1, -1))
    s_of, p_of = {}, {}
    for step in range(nq + 2):
        if step < nq:
            s_of[order[step]] = scores(order[step])
        if 0 <= step - 1 < nq:
            i = order[step - 1]
            p_of[i] = probs(i, s_of.pop(i))
        if 0 <= step - 2 < nq:
            i = order[step - 2]
            finish(i, *p_of.pop(i))


def _mla_attention(q_cat, k_cat, v, n_heads, *, blk):
    b, s, _ = q_cat.shape
    blk = _tile(s, blk)
    scale = (MLA_NOPE_DIM + MLA_ROPE_DIM) ** -0.5
    return pl.pallas_call(
        functools.partial(_mla_attn_kernel, blk=blk, scale=scale),
        out_shape=jax.ShapeDtypeStruct((b, s, n_heads * MLA_V_DIM), BF16),
        grid=(b, n_heads),
        in_specs=[
            pl.BlockSpec((None, s, MLA_QK_PAD), lambda bi, h: (bi, 0, h)),
            pl.BlockSpec((None, s, MLA_QK_PAD), lambda bi, h: (bi, 0, h)),
            pl.BlockSpec((None, s, MLA_V_DIM), lambda bi, h: (bi, 0, h)),
        ],
        out_specs=pl.BlockSpec((None, s, MLA_V_DIM), lambda bi, h: (bi, 0, h)),
        compiler_params=_params("parallel", "parallel"),
        name="mla_attention",
    )(q_cat, k_cat, v)


def _latent_kv_kernel(h_ref, g_ref, wdkv_ref, gl_ref, wukv_ref, ca_ref, cb_ref,
                      kcat_ref, v_ref, *, n_heads, rank):
    rows = h_ref.shape[0] // ROW_SPLITS
    for c in range(ROW_SPLITS):
        r = slice(c * rows, (c + 1) * rows)
        xn = _rms(h_ref[r, :], g_ref[...]).astype(BF16)
        ckv = jnp.dot(xn, wdkv_ref[...], preferred_element_type=F32)
        c_latent = _rms(ckv[:, :rank], gl_ref[...]).astype(BF16)
        k_rope = _rope(ckv[:, rank:], ca_ref[r, :], cb_ref[r, :]).astype(BF16)
        for h in range(n_heads):
            kv = jnp.dot(c_latent, wukv_ref[:, h * 256:(h + 1) * 256], preferred_element_type=F32)
            kcat_ref[r, h * MLA_QK_PAD:h * MLA_QK_PAD + MLA_NOPE_DIM] = kv[:, :MLA_NOPE_DIM].astype(BF16)
            kcat_ref[r, h * MLA_QK_PAD + MLA_NOPE_DIM:(h + 1) * MLA_QK_PAD] = k_rope
            v_ref[r, h * MLA_V_DIM:(h + 1) * MLA_V_DIM] = kv[:, MLA_NOPE_DIM:].astype(BF16)


def _latent_kv(h, g, w_dkv_pad, g_latent, w_ukv, ca, cb, n_heads, *, tm):
    m, d = h.shape
    rank = g_latent.shape[0]
    tm = _tile(m, tm)
    row = lambda i: (i, 0)
    fixed = lambda i: (0, 0)
    return pl.pallas_call(
        functools.partial(_latent_kv_kernel, n_heads=n_heads, rank=rank),
        out_shape=(jax.ShapeDtypeStruct((m, n_heads * MLA_QK_PAD), BF16),
                   jax.ShapeDtypeStruct((m, n_heads * MLA_V_DIM), BF16)),
        grid=(m // tm,),
        in_specs=[
            pl.BlockSpec((tm, d), row),
            pl.BlockSpec((1, d), fixed),
            pl.BlockSpec(w_dkv_pad.shape, fixed),
            pl.BlockSpec((1, rank), fixed),
            pl.BlockSpec(w_ukv.shape, fixed),
            pl.BlockSpec((tm, LANES), row),
            pl.BlockSpec((tm, LANES), row),
        ],
        out_specs=(pl.BlockSpec((tm, n_heads * MLA_QK_PAD), row),
                   pl.BlockSpec((tm, n_heads * MLA_V_DIM), row)),
        compiler_params=_params("parallel"),
        name="latent_kv",
    )(h, g.reshape(1, d), w_dkv_pad, g_latent.reshape(1, rank), w_ukv, ca, cb)


def _mla_in_kernel(h_ref, g_ref, win_ref, gq_ref, wuq_ref, ca_ref, cb_ref,
                   qcat_ref, memq_ref, *, n_heads, rank):
    rows = h_ref.shape[0] // ROW_SPLITS
    for c in range(ROW_SPLITS):
        r = slice(c * rows, (c + 1) * rows)
        xn = _rms(h_ref[r, :], g_ref[...]).astype(BF16)
        proj = jnp.dot(xn, win_ref[...], preferred_element_type=F32)
        memq_ref[r, :] = proj[:, rank:].astype(BF16)
        c_q = _rms(proj[:, :rank], gq_ref[...]).astype(BF16)
        ca, cb = ca_ref[r, :], cb_ref[r, :]
        for h in range(n_heads):
            q = jnp.dot(c_q, wuq_ref[:, h * MLA_QK_PAD:(h + 1) * MLA_QK_PAD], preferred_element_type=F32)
            qcat_ref[r, h * MLA_QK_PAD:h * MLA_QK_PAD + MLA_NOPE_DIM] = q[:, :MLA_NOPE_DIM].astype(BF16)
            qcat_ref[r, h * MLA_QK_PAD + MLA_NOPE_DIM:(h + 1) * MLA_QK_PAD] = (
                _rope(q[:, MLA_NOPE_DIM:], ca, cb).astype(BF16))


def _mla_in(h, g, w_in, g_q, w_uq_pad, layer, ca, cb, n_heads, *, tm):
    m, d = h.shape
    rank = g_q.shape[0]
    n_memq = w_in.shape[2] - rank
    tm = _tile(m, tm)
    row = lambda i: (i, 0)
    fixed = lambda i: (0, 0)
    stacked = lambda i: (layer, 0, 0)
    return pl.pallas_call(
        functools.partial(_mla_in_kernel, n_heads=n_heads, rank=rank),
        out_shape=(jax.ShapeDtypeStruct((m, n_heads * MLA_QK_PAD), BF16),
                   jax.ShapeDtypeStruct((m, n_memq), BF16)),
        grid=(m // tm,),
        in_specs=[
            pl.BlockSpec((tm, d), row),
            pl.BlockSpec((1, d), fixed),
            pl.BlockSpec((None,) + w_in.shape[1:], stacked),
            pl.BlockSpec((1, rank), fixed),
            pl.BlockSpec((None,) + w_uq_pad.shape[1:], stacked),
            pl.BlockSpec((tm, LANES), row),
            pl.BlockSpec((tm, LANES), row),
        ],
        out_specs=(pl.BlockSpec((tm, n_heads * MLA_QK_PAD), row),
                   pl.BlockSpec((tm, n_memq), row)),
        compiler_params=_params("parallel"),
        name="mla_in",
    )(h, g.reshape(1, d), w_in, g_q.reshape(1, rank), w_uq_pad, ca, cb)


def _rope_tables(positions):
    half = MLA_ROPE_DIM // 2
    inv_freq = ROPE_THETA ** (-jnp.arange(half, dtype=F32) / half)
    ang = positions.astype(F32)[..., None] * inv_freq
    cos, sin = jnp.cos(ang), jnp.sin(ang)
    zeros = jnp.zeros(cos.shape[:-1] + (LANES - MLA_ROPE_DIM,), F32)
    ca = jnp.concatenate([cos, cos, zeros], axis=-1)
    cb = jnp.concatenate([-sin, sin, zeros], axis=-1)
    return ca.reshape(-1, LANES), cb.reshape(-1, LANES)


def kernel(x, mem, positions, attn_norm_g, ffn_norm_g, a_w_in, a_w_out, b_w_in, b_q_norm_g, b_w_uq, b_w_out, mem_norm_g, w_mem_kv, kv_norm_g, w_dkv, kv_latent_g, w_ukv, ffn_w_gu, ffn_w_down, final_norm_g):
    b, s, d = x.shape
    mem_len = mem.shape[1]
    depth = attn_norm_g.shape[0]
    n_a = a_w_in.shape[0]
    n_b = b_w_in.shape[0]
    mq_w = MEM_HEADS * HEAD_DIM
    sb_heads = (a_w_in.shape[2] - mq_w) // (3 * HEAD_DIM)
    q_rank = b_q_norm_g.shape[1]
    kv_rank = kv_latent_g.shape[0]
    mla_heads = b_w_uq.shape[2] // (MLA_NOPE_DIM + MLA_ROPE_DIM)

    a_w_in_b = a_w_in.astype(BF16)
    a_w_out_b = a_w_out.astype(BF16)
    b_w_in_b = b_w_in.astype(BF16)
    b_w_out_b = b_w_out.astype(BF16)
    w_gu_b = ffn_w_gu.astype(BF16)
    w_d_b = ffn_w_down.astype(BF16)
    w_ukv_b = w_ukv.astype(BF16)
    w_mem_all = jnp.transpose(w_mem_kv, (1, 0, 2)).reshape(d, depth * 2 * mq_w).astype(BF16)
    w_dkv_pad = jnp.pad(w_dkv, ((0, 0), (0, LANES - MLA_ROPE_DIM))).astype(BF16)
    w_uq_pad = jnp.pad(
        b_w_uq.reshape(n_b, q_rank, mla_heads, MLA_NOPE_DIM + MLA_ROPE_DIM),
        ((0, 0), (0, 0), (0, 0), (0, MLA_QK_PAD - MLA_NOPE_DIM - MLA_ROPE_DIM)),
    ).reshape(n_b, q_rank, mla_heads * MLA_QK_PAD).astype(BF16)
    ca, cb = _rope_tables(positions)
    a_col_scale = jnp.concatenate([jnp.full((sb_heads * HEAD_DIM,), SB_Q_PRESCALE, F32),
                                   jnp.ones((a_w_in.shape[2] - sb_heads * HEAD_DIM,), F32)])

    h = x.reshape(b * s, d)
    mkv = _norm_matmul(mem.reshape(b * mem_len, d), mem_norm_g, w_mem_all[None], 0,
                       jnp.ones((w_mem_all.shape[1],), F32), tm=1024, tn=1024)
    mkv = mkv.reshape(b, mem_len, depth * 2 * mq_w)

    k_cat = v_lat = None
    for layer in range(depth):
        if layer == n_a:
            k_cat, v_lat = _latent_kv(h, kv_norm_g, w_dkv_pad, kv_latent_g, w_ukv_b, ca, cb,
                                      mla_heads, tm=1024)
            k_cat = k_cat.reshape(b, s, -1)
            v_lat = v_lat.reshape(b, s, -1)
        if layer < n_a:
            proj = _norm_matmul(h, attn_norm_g[layer], a_w_in_b, layer, a_col_scale, tm=1024, tn=1024)
            mix = _sb_attention(proj.reshape(b, s, -1), sb_heads, blk=256)
            q_src, q_block = proj, (3 * sb_heads * HEAD_DIM) // mq_w
            w_out, li = a_w_out_b, layer
        else:
            li = layer - n_a
            q_cat, mem_q = _mla_in(h, attn_norm_g[layer], b_w_in_b, b_q_norm_g[li], w_uq_pad, li,
                                   ca, cb, mla_heads, tm=1024)
            mix = _mla_attention(q_cat.reshape(b, s, -1), k_cat, v_lat, mla_heads, blk=256)
            q_src, q_block = mem_q, 0
            w_out = b_w_out_b
        h = _mem_out_proj(mix.reshape(b * s, -1), q_src, q_block, mkv, 2 * layer, w_out, li, h, tm=512)
        h = _ffn(h, ffn_norm_g[layer], w_gu_b, w_d_b, layer, final_norm_g,
                 final_norm=(layer == depth - 1), tm=1024, tf=512)
    return h.reshape(b, s, d)
```

```python
import functools

import jax
import jax.numpy as jnp
from jax import lax
from jax.experimental import pallas as pl
from jax.experimental.pallas import tpu as pltpu

HEAD_DIM = 128
MEM_HEADS = 4
MLA_NOPE_DIM = 128
MLA_ROPE_DIM = 64
MLA_V_DIM = 128
ROPE_THETA = 10000.0
RMS_EPS = 1e-6
LOG2_E = 1.4426950408889634
SB_Q_PRESCALE = -(HEAD_DIM ** -0.5) * LOG2_E
SB_NEAR_KEY_BLOCKS = 2
SB_ZERO_WEIGHT_LOG2 = -140.0

LANES = 128
MLA_QK_PAD = 2 * LANES
VMEM_LIMIT_BYTES = 48 * 1024 * 1024
NORM_CHUNK_ROWS = 256
FFN_FIRST_STEP_SPLITS = 2
NORM_MATMUL_FIRST_STEP_SPLITS = 2
ROW_SPLITS = 2

BF16 = jnp.bfloat16
F32 = jnp.float32


VMEM_LIMIT_BYTES_FFN = 58 * 1024 * 1024


def _params(*semantics, vmem_limit_bytes=VMEM_LIMIT_BYTES):
    return pltpu.CompilerParams(dimension_semantics=semantics, vmem_limit_bytes=vmem_limit_bytes)


def _tile(n, pref):
    t = min(n, pref)
    assert n % t == 0, (n, pref)
    return t


def _rms(x, g):
    return (x * lax.rsqrt(jnp.mean(x * x, axis=-1, keepdims=True) + RMS_EPS)) * g


def _rmsnorm_rows(x_ref, g_ref, out_ref):
    rows = x_ref.shape[0]
    chunk = min(rows, NORM_CHUNK_ROWS)
    g = g_ref[...]

    def body(c, carry):
        r = pl.multiple_of(c * chunk, chunk)
        out_ref[pl.ds(r, chunk), :] = _rms(x_ref[pl.ds(r, chunk), :], g).astype(out_ref.dtype)
        return carry

    lax.fori_loop(0, rows // chunk, body, 0)


def _rope(x, ca, cb):
    partner = pltpu.roll(x, 96, 1) + pltpu.roll(x, 32, 1)
    return x * ca + partner * cb


def _norm_matmul_kernel(h_ref, g_ref, w_ref, cs_ref, o_ref, xn_ref, *, tn):
    def column_tile(xn, j):
        cols = slice(j * tn, (j + 1) * tn)
        return (jnp.dot(xn, w_ref[:, cols], preferred_element_type=F32) * cs_ref[:, cols]).astype(o_ref.dtype)

    rows = h_ref.shape[0] // NORM_MATMUL_FIRST_STEP_SPLITS
    g = g_ref[...]
    for c in range(NORM_MATMUL_FIRST_STEP_SPLITS):
        r = slice(c * rows, (c + 1) * rows)
        xn = _rms(h_ref[r, :], g).astype(BF16)
        xn_ref[r, :] = xn
        o_ref[r, 0:tn] = column_tile(xn, 0)
    for j in range(1, o_ref.shape[1] // tn):
        o_ref[:, j * tn:(j + 1) * tn] = column_tile(xn_ref[...], j)


def _norm_matmul(h, g, w, layer, col_scale, *, tm, tn):
    m, d = h.shape
    n = w.shape[2]
    tm, tn = _tile(m, tm), _tile(n, tn)
    return pl.pallas_call(
        functools.partial(_norm_matmul_kernel, tn=tn),
        out_shape=jax.ShapeDtypeStruct((m, n), BF16),
        grid=(m // tm,),
        in_specs=[
            pl.BlockSpec((tm, d), lambda i: (i, 0)),
            pl.BlockSpec((1, d), lambda i: (0, 0)),
            pl.BlockSpec((None, d, n), lambda i: (layer, 0, 0), pipeline_mode=pl.Buffered(1)),
            pl.BlockSpec((1, n), lambda i: (0, 0)),
        ],
        out_specs=pl.BlockSpec((tm, n), lambda i: (i, 0)),
        scratch_shapes=[pltpu.VMEM((tm, d), BF16)],
        compiler_params=_params("parallel", vmem_limit_bytes=VMEM_LIMIT_BYTES_FFN),
        name="norm_matmul",
    )(h, g.reshape(1, d), w, col_scale.reshape(1, n))


def _ffn_kernel(h_ref, g_ref, wg_ref, wu_ref, wd_ref, gf_ref, o_ref, xn_ref, *, final_norm):
    j = pl.program_id(1)

    def hidden_tile(xn):
        gate = jnp.dot(xn, wg_ref[...], preferred_element_type=F32)
        up = jnp.dot(xn, wu_ref[...], preferred_element_type=F32)
        act = (gate * jax.nn.sigmoid(gate) * up).astype(BF16)
        return jnp.dot(act, wd_ref[...], preferred_element_type=F32)

    @pl.when(j == 0)
    def _():
        half = h_ref.shape[0] // FFN_FIRST_STEP_SPLITS
        g = g_ref[...]
        for c in range(FFN_FIRST_STEP_SPLITS):
            h = h_ref[c * half:(c + 1) * half, :]
            xn = _rms(h, g).astype(BF16)
            xn_ref[c * half:(c + 1) * half, :] = xn
            o_ref[c * half:(c + 1) * half, :] = h + hidden_tile(xn)

    @pl.when(j > 0)
    def _():
        o_ref[...] += hidden_tile(xn_ref[...])

    if final_norm:
        @pl.when(j == pl.num_programs(1) - 1)
        def _():
            _rmsnorm_rows(o_ref, gf_ref, o_ref)


def _ffn(h, g, w_gu, w_d, layer, gf, *, final_norm, tm, tf):
    m, d = h.shape
    f = w_d.shape[1]
    tm, tf = _tile(m, tm), _tile(f, tf)
    nf = f // tf
    return pl.pallas_call(
        functools.partial(_ffn_kernel, final_norm=final_norm),
        out_shape=jax.ShapeDtypeStruct((m, d), F32),
        grid=(m // tm, nf),
        in_specs=[
            pl.BlockSpec((tm, d), lambda i, j: (i, 0)),
            pl.BlockSpec((1, d), lambda i, j: (0, 0)),
            pl.BlockSpec((None, d, tf), lambda i, j: (layer, 0, j)),
            pl.BlockSpec((None, d, tf), lambda i, j: (layer, 0, nf + j)),
            pl.BlockSpec((None, tf, d), lambda i, j: (layer, j, 0)),
            pl.BlockSpec((1, d), lambda i, j: (0, 0)),
        ],
        out_specs=pl.BlockSpec((tm, d), lambda i, j: (i, 0)),
        scratch_shapes=[pltpu.VMEM((tm, d), BF16)],
        compiler_params=_params("parallel", "arbitrary", vmem_limit_bytes=VMEM_LIMIT_BYTES_FFN),
        name="ffn",
    )(h, g.reshape(1, d), w_gu, w_gu, w_d, gf.reshape(1, d))


def _mem_out_proj_kernel(mix_ref, qm_ref, mk_ref, mv_ref, w1_ref, w2_ref, h_ref, o_ref, *, scale):
    acc = h_ref[...] + jnp.dot(mix_ref[...], w1_ref[...], preferred_element_type=F32)
    heads = []
    for hd in range(MEM_HEADS):
        cols = slice(hd * HEAD_DIM, (hd + 1) * HEAD_DIM)
        s = lax.dot_general(qm_ref[:, cols], mk_ref[:, cols], (((1,), (1,)), ((), ())),
                            preferred_element_type=F32)
        m = jnp.max(s, axis=1, keepdims=True)
        p = jnp.exp2((s - m) * (scale * LOG2_E))
        l = jnp.sum(p, axis=1, keepdims=True)
        o = jnp.dot(p.astype(BF16), mv_ref[:, cols], preferred_element_type=F32)
        heads.append((o / l).astype(BF16))
    mem_out = jnp.concatenate(heads, axis=1)
    o_ref[...] = acc + jnp.dot(mem_out, w2_ref[...], preferred_element_type=F32)


def _mem_out_proj(mix, q_src, q_block, mkv, kv_block, w_out, layer, h, *, tm):
    m, d = h.shape
    bsz, ml, _ = mkv.shape
    k1, k2 = mix.shape[1], MEM_HEADS * HEAD_DIM
    assert k1 % k2 == 0 and w_out.shape[1] == k1 + k2
    tm = _tile(m // bsz, tm)
    steps_per_batch = (m // bsz) // tm
    return pl.pallas_call(
        functools.partial(_mem_out_proj_kernel, scale=HEAD_DIM ** -0.5),
        out_shape=jax.ShapeDtypeStruct((m, d), F32),
        grid=(m // tm,),
        in_specs=[
            pl.BlockSpec((tm, k1), lambda i: (i, 0)),
            pl.BlockSpec((tm, k2), lambda i: (i, q_block)),
            pl.BlockSpec((None, ml, k2), lambda i: (i // steps_per_batch, 0, kv_block)),
            pl.BlockSpec((None, ml, k2), lambda i: (i // steps_per_batch, 0, kv_block + 1)),
            pl.BlockSpec((None, k1, d), lambda i: (layer, 0, 0)),
            pl.BlockSpec((None, k2, d), lambda i: (layer, k1 // k2, 0)),
            pl.BlockSpec((tm, d), lambda i: (i, 0)),
        ],
        out_specs=pl.BlockSpec((tm, d), lambda i: (i, 0)),
        compiler_params=_params("parallel"),
        name="mem_out_proj",
    )(mix, q_src, mkv, mkv, w_out, w_out, h)


def _sb_attn_kernel(q_ref, k_ref, v_ref, tri_ref, o_ref, acc_ref, carry_ref, *, blk):
    tri2 = tri_ref[...]
    row = lax.broadcasted_iota(jnp.int32, (blk, blk), 0)
    col = lax.broadcasted_iota(jnp.int32, (blk, blk), 1)
    strictly_past = col < row
    nq = q_ref.shape[0] // blk

    def rows(i):
        return slice(i * blk, (i + 1) * blk)

    def log_keep(nz):
        neg_abs = lax.bitcast_convert_type(
            lax.bitcast_convert_type(nz, jnp.uint32) | jnp.uint32(0x80000000), F32)
        return jnp.minimum(nz, 0.0) - jnp.log2(1.0 + jnp.exp2(neg_abs))

    def block_weights(nz, lk, carry, diagonal):
        if diagonal:
            lk = jnp.where(strictly_past, lk, 0.0)
        hi = lk.astype(BF16)
        lo = (lk - hi.astype(F32)).astype(BF16)
        log_w = jnp.dot(jnp.concatenate([hi, lo], axis=1), tri2, preferred_element_type=F32) - nz
        if carry is not None:
            log_w = log_w + carry
        w = jnp.exp2(log_w)
        if diagonal:
            w = jnp.where(strictly_past, w, 0.0)
        block_sum = jnp.sum(lk, axis=1, keepdims=True)
        return w.astype(BF16), (block_sum if carry is None else carry + block_sum)

    def first_key_block(i):
        return max(i - SB_NEAR_KEY_BLOCKS + 1, 0)

    def logits(i):
        keys = slice(first_key_block(i) * blk, (i + 1) * blk)
        return lax.dot_general(q_ref[rows(i), :], k_ref[keys, :], (((1,), (1,)), ((), ())),
                               preferred_element_type=F32)

    def finish(i, nz, lkeep):
        j0 = first_key_block(i)
        w_blocks, carry = [None] * (i + 1 - j0), None
        for j in range(i, j0 - 1, -1):
            cols = slice((j - j0) * blk, (j - j0 + 1) * blk)
            w_blocks[j - j0], carry = block_weights(nz[:, cols], lkeep[:, cols], carry, j == i)
        w = w_blocks[0] if len(w_blocks) == 1 else jnp.concatenate(w_blocks, axis=1)
        acc = jnp.dot(w, v_ref[j0 * blk:(i + 1) * blk, :], preferred_element_type=F32)
        o_ref[rows(i), :] = acc.astype(o_ref.dtype)
        if j0 > 0:
            acc_ref[rows(i), :] = acc
            carry_ref[rows(i), :] = jnp.broadcast_to(carry, (blk, LANES))
            q32 = q_ref[rows(i), :].astype(F32)
            logit_bound[i] = jnp.sqrt(jnp.max(jnp.sum(q32 * q32, axis=1, keepdims=True)) * k_sq) * 1.01 + 1.0
            carry_max[i] = jnp.max(carry)

    if nq > SB_NEAR_KEY_BLOCKS:
        k32 = k_ref[...].astype(F32)
        k_sq = jnp.max(jnp.sum(k32 * k32, axis=1, keepdims=True))
    logit_bound, carry_max = {}, {}

    order = list(range(nq - 1, -1, -1))
    nz_of, lk_of = {}, {}
    for step in range(nq + 2):
        if step < nq:
            nz_of[order[step]] = logits(order[step])
        if 0 <= step - 1 < nq:
            lk_of[order[step - 1]] = log_keep(nz_of[order[step - 1]])
        if 0 <= step - 2 < nq:
            i = order[step - 2]
            finish(i, nz_of.pop(i), lk_of.pop(i))

    for i in range(SB_NEAR_KEY_BLOCKS, nq):
        def more(state, i=i):
            j, c_max = state
            return jnp.logical_and(j >= 0, c_max + logit_bound[i] >= SB_ZERO_WEIGHT_LOG2)

        def take_block(state, i=i):
            j, _ = state
            keys = pl.ds(pl.multiple_of(j * blk, blk), blk)
            nz = lax.dot_general(q_ref[rows(i), :], k_ref[keys, :], (((1,), (1,)), ((), ())),
                                 preferred_element_type=F32)
            w, carry = block_weights(nz, log_keep(nz), carry_ref[rows(i), 0:1], False)
            acc_ref[rows(i), :] += jnp.dot(w, v_ref[keys, :], preferred_element_type=F32)
            carry_ref[rows(i), :] = jnp.broadcast_to(carry, (blk, LANES))
            return j - 1, jnp.max(carry)

        j_end, _ = lax.while_loop(more, take_block, (jnp.int32(i - SB_NEAR_KEY_BLOCKS), carry_max[i]))

        @pl.when(j_end < i - SB_NEAR_KEY_BLOCKS)
        def _(i=i):
            o_ref[rows(i), :] = acc_ref[rows(i), :].astype(o_ref.dtype)


def _sb_attention(proj, n_heads, *, blk):
    b, s, _ = proj.shape
    blk = _tile(s, blk)
    tri = (jnp.arange(blk)[:, None] >= jnp.arange(blk)[None, :]).astype(BF16)
    tri2 = jnp.concatenate([tri, tri], axis=0)
    return pl.pallas_call(
        functools.partial(_sb_attn_kernel, blk=blk),
        out_shape=jax.ShapeDtypeStruct((b, s, n_heads * HEAD_DIM), BF16),
        grid=(b, n_heads),
        in_specs=[
            pl.BlockSpec((None, s, HEAD_DIM), lambda bi, h: (bi, 0, h)),
            pl.BlockSpec((None, s, HEAD_DIM), lambda bi, h: (bi, 0, n_heads + h)),
            pl.BlockSpec((None, s, HEAD_DIM), lambda bi, h: (bi, 0, 2 * n_heads + h)),
            pl.BlockSpec((2 * blk, blk), lambda bi, h: (0, 0)),
        ],
        out_specs=pl.BlockSpec((None, s, HEAD_DIM), lambda bi, h: (bi, 0, h)),
        scratch_shapes=[pltpu.VMEM((s, HEAD_DIM), F32), pltpu.VMEM((s, LANES), F32)],
        compiler_params=_params("parallel", "parallel"),
        name="sb_attention",
    )(proj, proj, proj, tri2)


def _mla_attn_kernel(q_ref, k_ref, v_ref, o_ref, *, blk, scale):
    row = lax.broadcasted_iota(jnp.int32, (blk, blk), 0)
    col = lax.broadcasted_iota(jnp.int32, (blk, blk), 1)
    causal = col <= row
    nq = q_ref.shape[0] // blk

    def scores(i):
        q = q_ref[i * blk:(i + 1) * blk, :]
        return lax.dot_general(q, k_ref[0:(i + 1) * blk, :], (((1,), (1,)), ((), ())),
                               preferred_element_type=F32)

    def probs(i, s):
        diag = jnp.where(causal, s[:, i * blk:], -jnp.inf)
        s = diag if i == 0 else jnp.concatenate([s[:, :i * blk], diag], axis=1)
        m = jnp.max(s, axis=1, keepdims=True)
        p = jnp.exp2((s - m) * (scale * LOG2_E))
        return p.astype(BF16), jnp.sum(p, axis=1, keepdims=True)

    def finish(i, p, l):
        o = jnp.dot(p, v_ref[0:(i + 1) * blk, :], preferred_element_type=F32)
        o_ref[i * blk:(i + 1) * blk, :] = (o / l).astype(o_ref.dtype)

    order = list(range(nq - 1, -1, -1))
    s_of, p_of = {}, {}
    for step in range(nq + 2):
        if step < nq:
            s_of[order[step]] = scores(order[step])
        if 0 <= step - 1 < nq:
            i = order[step - 1]
            p_of[i] = probs(i, s_of.pop(i))
        if 0 <= step - 2 < nq:
            i = order[step - 2]
            finish(i, *p_of.pop(i))


def _mla_attention(q_cat, k_cat, v, n_heads, *, blk):
    b, s, _ = q_cat.shape
    blk = _tile(s, blk)
    scale = (MLA_NOPE_DIM + MLA_ROPE_DIM) ** -0.5
    return pl.pallas_call(
        functools.partial(_mla_attn_kernel, blk=blk, scale=scale),
        out_shape=jax.ShapeDtypeStruct((b, s, n_heads * MLA_V_DIM), BF16),
        grid=(b, n_heads),
        in_specs=[
            pl.BlockSpec((None, s, MLA_QK_PAD), lambda bi, h: (bi, 0, h)),
            pl.BlockSpec((None, s, MLA_QK_PAD), lambda bi, h: (bi, 0, h)),
            pl.BlockSpec((None, s, MLA_V_DIM), lambda bi, h: (bi, 0, h)),
        ],
        out_specs=pl.BlockSpec((None, s, MLA_V_DIM), lambda bi, h: (bi, 0, h)),
        compiler_params=_params("parallel", "parallel"),
        name="mla_attention",
    )(q_cat, k_cat, v)


def _latent_kv_kernel(h_ref, g_ref, wdkv_ref, gl_ref, wukv_ref, ca_ref, cb_ref,
                      kcat_ref, v_ref, *, n_heads, rank):
    rows = h_ref.shape[0] // ROW_SPLITS
    for c in range(ROW_SPLITS):
        r = slice(c * rows, (c + 1) * rows)
        xn = _rms(h_ref[r, :], g_ref[...]).astype(BF16)
        ckv = jnp.dot(xn, wdkv_ref[...], preferred_element_type=F32)
        c_latent = _rms(ckv[:, :rank], gl_ref[...]).astype(BF16)
        k_rope = _rope(ckv[:, rank:], ca_ref[r, :], cb_ref[r, :]).astype(BF16)
        for h in range(n_heads):
            kv = jnp.dot(c_latent, wukv_ref[:, h * 256:(h + 1) * 256], preferred_element_type=F32)
            kcat_ref[r, h * MLA_QK_PAD:h * MLA_QK_PAD + MLA_NOPE_DIM] = kv[:, :MLA_NOPE_DIM].astype(BF16)
            kcat_ref[r, h * MLA_QK_PAD + MLA_NOPE_DIM:(h + 1) * MLA_QK_PAD] = k_rope
            v_ref[r, h * MLA_V_DIM:(h + 1) * MLA_V_DIM] = kv[:, MLA_NOPE_DIM:].astype(BF16)


def _latent_kv(h, g, w_dkv_pad, g_latent, w_ukv, ca, cb, n_heads, *, tm):
    m, d = h.shape
    rank = g_latent.shape[0]
    tm = _tile(m, tm)
    row = lambda i: (i, 0)
    fixed = lambda i: (0, 0)
    return pl.pallas_call(
        functools.partial(_latent_kv_kernel, n_heads=n_heads, rank=rank),
        out_shape=(jax.ShapeDtypeStruct((m, n_heads * MLA_QK_PAD), BF16),
                   jax.ShapeDtypeStruct((m, n_heads * MLA_V_DIM), BF16)),
        grid=(m // tm,),
        in_specs=[
            pl.BlockSpec((tm, d), row),
            pl.BlockSpec((1, d), fixed),
            pl.BlockSpec(w_dkv_pad.shape, fixed),
            pl.BlockSpec((1, rank), fixed),
            pl.BlockSpec(w_ukv.shape, fixed),
            pl.BlockSpec((tm, LANES), row),
            pl.BlockSpec((tm, LANES), row),
        ],
        out_specs=(pl.BlockSpec((tm, n_heads * MLA_QK_PAD), row),
                   pl.BlockSpec((tm, n_heads * MLA_V_DIM), row)),
        compiler_params=_params("parallel"),
        name="latent_kv",
    )(h, g.reshape(1, d), w_dkv_pad, g_latent.reshape(1, rank), w_ukv, ca, cb)


def _mla_in_kernel(h_ref, g_ref, win_ref, gq_ref, wuq_ref, ca_ref, cb_ref,
                   qcat_ref, memq_ref, *, n_heads, rank):
    rows = h_ref.shape[0] // ROW_SPLITS
    for c in range(ROW_SPLITS):
        r = slice(c * rows, (c + 1) * rows)
        xn = _rms(h_ref[r, :], g_ref[...]).astype(BF16)
        proj = jnp.dot(xn, win_ref[...], preferred_element_type=F32)
        memq_ref[r, :] = proj[:, rank:].astype(BF16)
        c_q = _rms(proj[:, :rank], gq_ref[...]).astype(BF16)
        ca, cb = ca_ref[r, :], cb_ref[r, :]
        for h in range(n_heads):
            q = jnp.dot(c_q, wuq_ref[:, h * MLA_QK_PAD:(h + 1) * MLA_QK_PAD], preferred_element_type=F32)
            qcat_ref[r, h * MLA_QK_PAD:h * MLA_QK_PAD + MLA_NOPE_DIM] = q[:, :MLA_NOPE_DIM].astype(BF16)
            qcat_ref[r, h * MLA_QK_PAD + MLA_NOPE_DIM:(h + 1) * MLA_QK_PAD] = (
                _rope(q[:, MLA_NOPE_DIM:], ca, cb).astype(BF16))


def _mla_in(h, g, w_in, g_q, w_uq_pad, layer, ca, cb, n_heads, *, tm):
    m, d = h.shape
    rank = g_q.shape[0]
    n_memq = w_in.shape[2] - rank
    tm = _tile(m, tm)
    row = lambda i: (i, 0)
    fixed = lambda i: (0, 0)
    stacked = lambda i: (layer, 0, 0)
    return pl.pallas_call(
        functools.partial(_mla_in_kernel, n_heads=n_heads, rank=rank),
        out_shape=(jax.ShapeDtypeStruct((m, n_heads * MLA_QK_PAD), BF16),
                   jax.ShapeDtypeStruct((m, n_memq), BF16)),
        grid=(m // tm,),
        in_specs=[
            pl.BlockSpec((tm, d), row),
            pl.BlockSpec((1, d), fixed),
            pl.BlockSpec((None,) + w_in.shape[1:], stacked),
            pl.BlockSpec((1, rank), fixed),
            pl.BlockSpec((None,) + w_uq_pad.shape[1:], stacked),
            pl.BlockSpec((tm, LANES), row),
            pl.BlockSpec((tm, LANES), row),
        ],
        out_specs=(pl.BlockSpec((tm, n_heads * MLA_QK_PAD), row),
                   pl.BlockSpec((tm, n_memq), row)),
        compiler_params=_params("parallel"),
        name="mla_in",
    )(h, g.reshape(1, d), w_in, g_q.reshape(1, rank), w_uq_pad, ca, cb)


def _rope_tables(positions):
    half = MLA_ROPE_DIM // 2
    inv_freq = ROPE_THETA ** (-jnp.arange(half, dtype=F32) / half)
    ang = positions.astype(F32)[..., None] * inv_freq
    cos, sin = jnp.cos(ang), jnp.sin(ang)
    zeros = jnp.zeros(cos.shape[:-1] + (LANES - MLA_ROPE_DIM,), F32)
    ca = jnp.concatenate([cos, cos, zeros], axis=-1)
    cb = jnp.concatenate([-sin, sin, zeros], axis=-1)
    return ca.reshape(-1, LANES), cb.reshape(-1, LANES)


def kernel(x, mem, positions, attn_norm_g, ffn_norm_g, a_w_in, a_w_out, b_w_in, b_q_norm_g, b_w_uq, b_w_out, mem_norm_g, w_mem_kv, kv_norm_g, w_dkv, kv_latent_g, w_ukv, ffn_w_gu, ffn_w_down, final_norm_g):
    b, s, d = x.shape
    mem_len = mem.shape[1]
    depth = attn_norm_g.shape[0]
    n_a = a_w_in.shape[0]
    n_b = b_w_in.shape[0]
    mq_w = MEM_HEADS * HEAD_DIM
    sb_heads = (a_w_in.shape[2] - mq_w) // (3 * HEAD_DIM)
    q_rank = b_q_norm_g.shape[1]
    kv_rank = kv_latent_g.shape[0]
    mla_heads = b_w_uq.shape[2] // (MLA_NOPE_DIM + MLA_ROPE_DIM)

    a_w_in_b = a_w_in.astype(BF16)
    a_w_out_b = a_w_out.astype(BF16)
    b_w_in_b = b_w_in.astype(BF16)
    b_w_out_b = b_w_out.astype(BF16)
    w_gu_b = ffn_w_gu.astype(BF16)
    w_d_b = ffn_w_down.astype(BF16)
    w_ukv_b = w_ukv.astype(BF16)
    w_mem_all = jnp.transpose(w_mem_kv, (1, 0, 2)).reshape(d, depth * 2 * mq_w).astype(BF16)
    w_dkv_pad = jnp.pad(w_dkv, ((0, 0), (0, LANES - MLA_ROPE_DIM))).astype(BF16)
    w_uq_pad = jnp.pad(
        b_w_uq.reshape(n_b, q_rank, mla_heads, MLA_NOPE_DIM + MLA_ROPE_DIM),
        ((0, 0), (0, 0), (0, 0), (0, MLA_QK_PAD - MLA_NOPE_DIM - MLA_ROPE_DIM)),
    ).reshape(n_b, q_rank, mla_heads * MLA_QK_PAD).astype(BF16)
    ca, cb = _rope_tables(positions)
    a_col_scale = jnp.concatenate([jnp.full((sb_heads * HEAD_DIM,), SB_Q_PRESCALE, F32),
                                   jnp.ones((a_w_in.shape[2] - sb_heads * HEAD_DIM,), F32)])

    h = x.reshape(b * s, d)
    mkv = _norm_matmul(mem.reshape(b * mem_len, d), mem_norm_g, w_mem_all[None], 0,
                       jnp.ones((w_mem_all.shape[1],), F32), tm=512, tn=1024)
    mkv = mkv.reshape(b, mem_len, depth * 2 * mq_w)

    k_cat = v_lat = None
    for layer in range(depth):
        if layer == n_a:
            k_cat, v_lat = _latent_kv(h, kv_norm_g, w_dkv_pad, kv_latent_g, w_ukv_b, ca, cb,
                                      mla_heads, tm=1024)
            k_cat = k_cat.reshape(b, s, -1)
            v_lat = v_lat.reshape(b, s, -1)
        if layer < n_a:
            proj = _norm_matmul(h, attn_norm_g[layer], a_w_in_b, layer, a_col_scale, tm=512, tn=1024)
            mix = _sb_attention(proj.reshape(b, s, -1), sb_heads, blk=256)
            q_src, q_block = proj, (3 * sb_heads * HEAD_DIM) // mq_w
            w_out, li = a_w_out_b, layer
        else:
            li = layer - n_a
            q_cat, mem_q = _mla_in(h, attn_norm_g[layer], b_w_in_b, b_q_norm_g[li], w_uq_pad, li,
                                   ca, cb, mla_heads, tm=1024)
            mix = _mla_attention(q_cat.reshape(b, s, -1), k_cat, v_lat, mla_heads, blk=256)
            q_src, q_block = mem_q, 0
            w_out = b_w_out_b
        h = _mem_out_proj(mix.reshape(b * s, -1), q_src, q_block, mkv, 2 * layer, w_out, li, h, tm=512)
        h = _ffn(h, ffn_norm_g[layer], w_gu_b, w_d_b, layer, final_norm_g,
                 final_norm=(layer == depth - 1), tm=1024, tf=512)
    return h.reshape(b, s, d)
```

```python
import functools

import jax
import jax.numpy as jnp
from jax import lax
from jax.experimental import pallas as pl
from jax.experimental.pallas import tpu as pltpu

HEAD_DIM = 128
MEM_HEADS = 4
MLA_NOPE_DIM = 128
MLA_ROPE_DIM = 64
MLA_V_DIM = 128
ROPE_THETA = 10000.0
RMS_EPS = 1e-6
LOG2_E = 1.4426950408889634
SB_Q_PRESCALE = -(HEAD_DIM ** -0.5) * LOG2_E
SB_NEAR_KEY_BLOCKS = 2
SB_ZERO_WEIGHT_LOG2 = -140.0

LANES = 128
MLA_QK_PAD = 2 * LANES
NORM_CHUNK_ROWS = 256
FFN_FIRST_STEP_SPLITS = 2
NORM_MATMUL_FIRST_STEP_SPLITS = 2
ROW_SPLITS = 2

VMEM_LIMIT_BYTES = 48 * 1024 * 1024
VMEM_LIMIT_BYTES_LARGE = 58 * 1024 * 1024
NORM_MATMUL_ROWS, NORM_MATMUL_COLS = 512, 1024
FFN_ROWS, FFN_HIDDEN = 1024, 512
MEM_OUT_PROJ_ROWS = 512
MLA_IN_ROWS = 1024
LATENT_KV_ROWS = 1024
ATTENTION_BLOCK = 256

BF16 = jnp.bfloat16
F32 = jnp.float32


def _params(*semantics, vmem_limit_bytes=VMEM_LIMIT_BYTES):
    return pltpu.CompilerParams(dimension_semantics=semantics, vmem_limit_bytes=vmem_limit_bytes)


def _tile(n, pref):
    t = min(n, pref)
    assert n % t == 0, (n, pref)
    return t


def _rms(x, g):
    return (x * lax.rsqrt(jnp.mean(x * x, axis=-1, keepdims=True) + RMS_EPS)) * g


def _rmsnorm_rows(x_ref, g_ref, out_ref):
    rows = x_ref.shape[0]
    chunk = min(rows, NORM_CHUNK_ROWS)
    g = g_ref[...]

    def body(c, carry):
        r = pl.multiple_of(c * chunk, chunk)
        out_ref[pl.ds(r, chunk), :] = _rms(x_ref[pl.ds(r, chunk), :], g).astype(out_ref.dtype)
        return carry

    lax.fori_loop(0, rows // chunk, body, 0)


def _rope(x, ca, cb):
    partner = pltpu.roll(x, 96, 1) + pltpu.roll(x, 32, 1)
    return x * ca + partner * cb


def _norm_matmul_kernel(h_ref, g_ref, w_ref, cs_ref, o_ref, xn_ref, *, tn):
    def column_tile(xn, j):
        cols = slice(j * tn, (j + 1) * tn)
        return (jnp.dot(xn, w_ref[:, cols], preferred_element_type=F32) * cs_ref[:, cols]).astype(o_ref.dtype)

    rows = h_ref.shape[0] // NORM_MATMUL_FIRST_STEP_SPLITS
    g = g_ref[...]
    for c in range(NORM_MATMUL_FIRST_STEP_SPLITS):
        r = slice(c * rows, (c + 1) * rows)
        xn = _rms(h_ref[r, :], g).astype(BF16)
        xn_ref[r, :] = xn
        o_ref[r, 0:tn] = column_tile(xn, 0)
    for j in range(1, o_ref.shape[1] // tn):
        o_ref[:, j * tn:(j + 1) * tn] = column_tile(xn_ref[...], j)


def _norm_matmul(h, g, w, layer, col_scale, *, tm, tn):
    m, d = h.shape
    n = w.shape[2]
    tm, tn = _tile(m, tm), _tile(n, tn)
    return pl.pallas_call(
        functools.partial(_norm_matmul_kernel, tn=tn),
        out_shape=jax.ShapeDtypeStruct((m, n), BF16),
        grid=(m // tm,),
        in_specs=[
            pl.BlockSpec((tm, d), lambda i: (i, 0)),
            pl.BlockSpec((1, d), lambda i: (0, 0)),
            pl.BlockSpec((None, d, n), lambda i: (layer, 0, 0), pipeline_mode=pl.Buffered(1)),
            pl.BlockSpec((1, n), lambda i: (0, 0)),
        ],
        out_specs=pl.BlockSpec((tm, n), lambda i: (i, 0)),
        scratch_shapes=[pltpu.VMEM((tm, d), BF16)],
        compiler_params=_params("parallel", vmem_limit_bytes=VMEM_LIMIT_BYTES_LARGE),
        name="norm_matmul",
    )(h, g.reshape(1, d), w, col_scale.reshape(1, n))


def _ffn_kernel(h_ref, g_ref, wg_ref, wu_ref, wd_ref, gf_ref, o_ref, xn_ref, *, final_norm):
    j = pl.program_id(1)

    def hidden_tile(xn):
        gate = jnp.dot(xn, wg_ref[...], preferred_element_type=F32)
        up = jnp.dot(xn, wu_ref[...], preferred_element_type=F32)
        act = (gate * jax.nn.sigmoid(gate) * up).astype(BF16)
        return jnp.dot(act, wd_ref[...], preferred_element_type=F32)

    @pl.when(j == 0)
    def _():
        half = h_ref.shape[0] // FFN_FIRST_STEP_SPLITS
        g = g_ref[...]
        for c in range(FFN_FIRST_STEP_SPLITS):
            h = h_ref[c * half:(c + 1) * half, :]
            xn = _rms(h, g).astype(BF16)
            xn_ref[c * half:(c + 1) * half, :] = xn
            o_ref[c * half:(c + 1) * half, :] = h + hidden_tile(xn)

    @pl.when(j > 0)
    def _():
        o_ref[...] += hidden_tile(xn_ref[...])

    if final_norm:
        @pl.when(j == pl.num_programs(1) - 1)
        def _():
            _rmsnorm_rows(o_ref, gf_ref, o_ref)


def _ffn(h, g, w_gu, w_d, layer, gf, *, final_norm, tm, tf):
    m, d = h.shape
    f = w_d.shape[1]
    tm, tf = _tile(m, tm), _tile(f, tf)
    nf = f // tf
    return pl.pallas_call(
        functools.partial(_ffn_kernel, final_norm=final_norm),
        out_shape=jax.ShapeDtypeStruct((m, d), F32),
        grid=(m // tm, nf),
        in_specs=[
            pl.BlockSpec((tm, d), lambda i, j: (i, 0)),
            pl.BlockSpec((1, d), lambda i, j: (0, 0)),
            pl.BlockSpec((None, d, tf), lambda i, j: (layer, 0, j)),
            pl.BlockSpec((None, d, tf), lambda i, j: (layer, 0, nf + j)),
            pl.BlockSpec((None, tf, d), lambda i, j: (layer, j, 0)),
            pl.BlockSpec((1, d), lambda i, j: (0, 0)),
        ],
        out_specs=pl.BlockSpec((tm, d), lambda i, j: (i, 0)),
        scratch_shapes=[pltpu.VMEM((tm, d), BF16)],
        compiler_params=_params("parallel", "arbitrary", vmem_limit_bytes=VMEM_LIMIT_BYTES_LARGE),
        name="ffn",
    )(h, g.reshape(1, d), w_gu, w_gu, w_d, gf.reshape(1, d))


def _mem_out_proj_kernel(mix_ref, qm_ref, mk_ref, mv_ref, w1_ref, w2_ref, h_ref, gu_ref, wd_ref,
                         o_ref, gu_b_ref, wd_b_ref, *, scale):
    gu_b_ref[...] = gu_ref[...].astype(BF16)
    wd_b_ref[...] = wd_ref[...].astype(BF16)
    acc = h_ref[...] + jnp.dot(mix_ref[...], w1_ref[...], preferred_element_type=F32)
    heads = []
    for hd in range(MEM_HEADS):
        cols = slice(hd * HEAD_DIM, (hd + 1) * HEAD_DIM)
        s = lax.dot_general(qm_ref[:, cols], mk_ref[:, cols], (((1,), (1,)), ((), ())),
                            preferred_element_type=F32)
        m = jnp.max(s, axis=1, keepdims=True)
        p = jnp.exp2((s - m) * (scale * LOG2_E))
        l = jnp.sum(p, axis=1, keepdims=True)
        o = jnp.dot(p.astype(BF16), mv_ref[:, cols], preferred_element_type=F32)
        heads.append((o / l).astype(BF16))
    mem_out = jnp.concatenate(heads, axis=1)
    o_ref[...] = acc + jnp.dot(mem_out, w2_ref[...], preferred_element_type=F32)


def _cast_rows(total_rows, steps):
    for rows in range(16, total_rows + 1, 16):
        if total_rows % rows == 0 and total_rows // rows <= steps:
            return rows
    raise ValueError((total_rows, steps))


def _mem_out_proj(mix, q_src, q_block, mkv, kv_block, w_out, layer, h, ffn_w_gu, ffn_w_d, ffn_layer, *, tm):
    m, d = h.shape
    bsz, ml, _ = mkv.shape
    k1, k2 = mix.shape[1], MEM_HEADS * HEAD_DIM
    assert k1 % k2 == 0 and w_out.shape[1] == k1 + k2
    tm = _tile(m // bsz, tm)
    steps_per_batch = (m // bsz) // tm
    steps = m // tm
    gu_rows, wd_rows = _cast_rows(ffn_w_gu.shape[1], steps), _cast_rows(ffn_w_d.shape[1], steps)
    gu_last, wd_last = ffn_w_gu.shape[1] // gu_rows - 1, ffn_w_d.shape[1] // wd_rows - 1
    return pl.pallas_call(
        functools.partial(_mem_out_proj_kernel, scale=HEAD_DIM ** -0.5),
        out_shape=(jax.ShapeDtypeStruct((m, d), F32),
                   jax.ShapeDtypeStruct(ffn_w_gu.shape[1:], BF16),
                   jax.ShapeDtypeStruct(ffn_w_d.shape[1:], BF16)),
        grid=(steps,),
        in_specs=[
            pl.BlockSpec((tm, k1), lambda i: (i, 0)),
            pl.BlockSpec((tm, k2), lambda i: (i, q_block)),
            pl.BlockSpec((None, ml, k2), lambda i: (i // steps_per_batch, 0, kv_block)),
            pl.BlockSpec((None, ml, k2), lambda i: (i // steps_per_batch, 0, kv_block + 1)),
            pl.BlockSpec((None, k1, d), lambda i: (layer, 0, 0)),
            pl.BlockSpec((None, k2, d), lambda i: (layer, k1 // k2, 0)),
            pl.BlockSpec((tm, d), lambda i: (i, 0)),
            pl.BlockSpec((None, gu_rows, ffn_w_gu.shape[2]), lambda i: (ffn_layer, jnp.minimum(i, gu_last), 0)),
            pl.BlockSpec((None, wd_rows, ffn_w_d.shape[2]), lambda i: (ffn_layer, jnp.minimum(i, wd_last), 0)),
        ],
        out_specs=(pl.BlockSpec((tm, d), lambda i: (i, 0)),
                   pl.BlockSpec((gu_rows, ffn_w_gu.shape[2]), lambda i: (jnp.minimum(i, gu_last), 0)),
                   pl.BlockSpec((wd_rows, ffn_w_d.shape[2]), lambda i: (jnp.minimum(i, wd_last), 0))),
        compiler_params=_params("arbitrary"),
        name="mem_out_proj",
    )(mix, q_src, mkv, mkv, w_out, w_out, h, ffn_w_gu, ffn_w_d)


def _sb_attn_kernel(q_ref, k_ref, v_ref, tri_ref, o_ref, acc_ref, carry_ref, *, blk):
    tri2 = tri_ref[...]
    row = lax.broadcasted_iota(jnp.int32, (blk, blk), 0)
    col = lax.broadcasted_iota(jnp.int32, (blk, blk), 1)
    strictly_past = col < row
    nq = q_ref.shape[0] // blk

    def rows(i):
        return slice(i * blk, (i + 1) * blk)

    def log_keep(nz):
        neg_abs = lax.bitcast_convert_type(
            lax.bitcast_convert_type(nz, jnp.uint32) | jnp.uint32(0x80000000), F32)
        return jnp.minimum(nz, 0.0) - jnp.log2(1.0 + jnp.exp2(neg_abs))

    def block_weights(nz, lk, carry, diagonal):
        if diagonal:
            lk = jnp.where(strictly_past, lk, 0.0)
        hi = lk.astype(BF16)
        lo = (lk - hi.astype(F32)).astype(BF16)
        log_w = jnp.dot(jnp.concatenate([hi, lo], axis=1), tri2, preferred_element_type=F32) - nz
        if carry is not None:
            log_w = log_w + carry
        w = jnp.exp2(log_w)
        if diagonal:
            w = jnp.where(strictly_past, w, 0.0)
        block_sum = jnp.sum(lk, axis=1, keepdims=True)
        return w.astype(BF16), (block_sum if carry is None else carry + block_sum)

    def first_key_block(i):
        return max(i - SB_NEAR_KEY_BLOCKS + 1, 0)

    def logits(i):
        keys = slice(first_key_block(i) * blk, (i + 1) * blk)
        return lax.dot_general(q_ref[rows(i), :], k_ref[keys, :], (((1,), (1,)), ((), ())),
                               preferred_element_type=F32)

    def finish(i, nz, lkeep):
        j0 = first_key_block(i)
        w_blocks, carry = [None] * (i + 1 - j0), None
        for j in range(i, j0 - 1, -1):
            cols = slice((j - j0) * blk, (j - j0 + 1) * blk)
            w_blocks[j - j0], carry = block_weights(nz[:, cols], lkeep[:, cols], carry, j == i)
        w = w_blocks[0] if len(w_blocks) == 1 else jnp.concatenate(w_blocks, axis=1)
        acc = jnp.dot(w, v_ref[j0 * blk:(i + 1) * blk, :], preferred_element_type=F32)
        o_ref[rows(i), :] = acc.astype(o_ref.dtype)
        if j0 > 0:
            acc_ref[rows(i), :] = acc
            carry_ref[rows(i), :] = jnp.broadcast_to(carry, (blk, LANES))
            q32 = q_ref[rows(i), :].astype(F32)
            logit_bound[i] = jnp.sqrt(jnp.max(jnp.sum(q32 * q32, axis=1, keepdims=True)) * k_sq) * 1.01 + 1.0
            carry_max[i] = jnp.max(carry)

    if nq > SB_NEAR_KEY_BLOCKS:
        k32 = k_ref[...].astype(F32)
        k_sq = jnp.max(jnp.sum(k32 * k32, axis=1, keepdims=True))
    logit_bound, carry_max = {}, {}

    order = list(range(nq - 1, -1, -1))
    nz_of, lk_of = {}, {}
    for step in range(nq + 2):
        if step < nq:
            nz_of[order[step]] = logits(order[step])
        if 0 <= step - 1 < nq:
            lk_of[order[step - 1]] = log_keep(nz_of[order[step - 1]])
        if 0 <= step - 2 < nq:
            i = order[step - 2]
            finish(i, nz_of.pop(i), lk_of.pop(i))

    for i in range(SB_NEAR_KEY_BLOCKS, nq):
        def more(state, i=i):
            j, c_max = state
            return jnp.logical_and(j >= 0, c_max + logit_bound[i] >= SB_ZERO_WEIGHT_LOG2)

        def take_block(state, i=i):
            j, _ = state
            keys = pl.ds(pl.multiple_of(j * blk, blk), blk)
            nz = lax.dot_general(q_ref[rows(i), :], k_ref[keys, :], (((1,), (1,)), ((), ())),
                                 preferred_element_type=F32)
            w, carry = block_weights(nz, log_keep(nz), carry_ref[rows(i), 0:1], False)
            acc_ref[rows(i), :] += jnp.dot(w, v_ref[keys, :], preferred_element_type=F32)
            carry_ref[rows(i), :] = jnp.broadcast_to(carry, (blk, LANES))
            return j - 1, jnp.max(carry)

        j_end, _ = lax.while_loop(more, take_block, (jnp.int32(i - SB_NEAR_KEY_BLOCKS), carry_max[i]))

        @pl.when(j_end < i - SB_NEAR_KEY_BLOCKS)
        def _(i=i):
            o_ref[rows(i), :] = acc_ref[rows(i), :].astype(o_ref.dtype)


def _sb_attention(proj, n_heads, *, blk):
    b, s, _ = proj.shape
    blk = _tile(s, blk)
    tri = (jnp.arange(blk)[:, None] >= jnp.arange(blk)[None, :]).astype(BF16)
    tri2 = jnp.concatenate([tri, tri], axis=0)
    return pl.pallas_call(
        functools.partial(_sb_attn_kernel, blk=blk),
        out_shape=jax.ShapeDtypeStruct((b, s, n_heads * HEAD_DIM), BF16),
        grid=(b, n_heads),
        in_specs=[
            pl.BlockSpec((None, s, HEAD_DIM), lambda bi, h: (bi, 0, h)),
            pl.BlockSpec((None, s, HEAD_DIM), lambda bi, h: (bi, 0, n_heads + h)),
            pl.BlockSpec((None, s, HEAD_DIM), lambda bi, h: (bi, 0, 2 * n_heads + h)),
            pl.BlockSpec((2 * blk, blk), lambda bi, h: (0, 0)),
        ],
        out_specs=pl.BlockSpec((None, s, HEAD_DIM), lambda bi, h: (bi, 0, h)),
        scratch_shapes=[pltpu.VMEM((s, HEAD_DIM), F32), pltpu.VMEM((s, LANES), F32)],
        compiler_params=_params("parallel", "parallel"),
        name="sb_attention",
    )(proj, proj, proj, tri2)


def _mla_attn_kernel(q_ref, k_ref, v_ref, o_ref, *, blk, scale):
    row = lax.broadcasted_iota(jnp.int32, (blk, blk), 0)
    col = lax.broadcasted_iota(jnp.int32, (blk, blk), 1)
    causal = col <= row
    nq = q_ref.shape[0] // blk

    def scores(i):
        q = q_ref[i * blk:(i + 1) * blk, :]
        return lax.dot_general(q, k_ref[0:(i + 1) * blk, :], (((1,), (1,)), ((), ())),
                               preferred_element_type=F32)

    def probs(i, s):
        diag = jnp.where(causal, s[:, i * blk:], -jnp.inf)
        s = diag if i == 0 else jnp.concatenate([s[:, :i * blk], diag], axis=1)
        m = jnp.max(s, axis=1, keepdims=True)
        p = jnp.exp2((s - m) * (scale * LOG2_E))
        return p.astype(BF16), jnp.sum(p, axis=1, keepdims=True)

    def finish(i, p, l):
        o = jnp.dot(p, v_ref[0:(i + 1) * blk, :], preferred_element_type=F32)
        o_ref[i * blk:(i + 1) * blk, :] = (o / l).astype(o_ref.dtype)

    order = list(range(nq - 1, -1, -1))
    s_of, p_of = {}, {}
    for step in range(nq + 2):
        if step < nq:
            s_of[order[step]] = scores(order[step])
        if 0 <= step - 1 < nq:
            i = order[step - 1]
            p_of[i] = probs(i, s_of.pop(i))
        if 0 <= step - 2 < nq:
            i = order[step - 2]
            finish(i, *p_of.pop(i))


def _mla_attention(q_cat, k_cat, v, n_heads, *, blk):
    b, s, _ = q_cat.shape
    blk = _tile(s, blk)
    scale = (MLA_NOPE_DIM + MLA_ROPE_DIM) ** -0.5
    return pl.pallas_call(
        functools.partial(_mla_attn_kernel, blk=blk, scale=scale),
        out_shape=jax.ShapeDtypeStruct((b, s, n_heads * MLA_V_DIM), BF16),
        grid=(b, n_heads),
        in_specs=[
            pl.BlockSpec((None, s, MLA_QK_PAD), lambda bi, h: (bi, 0, h)),
            pl.BlockSpec((None, s, MLA_QK_PAD), lambda bi, h: (bi, 0, h)),
            pl.BlockSpec((None, s, MLA_V_DIM), lambda bi, h: (bi, 0, h)),
        ],
        out_specs=pl.BlockSpec((None, s, MLA_V_DIM), lambda bi, h: (bi, 0, h)),
        compiler_params=_params("parallel", "parallel"),
        name="mla_attention",
    )(q_cat, k_cat, v)


def _latent_kv_kernel(h_ref, g_ref, wdkv_ref, gl_ref, wukv_ref, ca_ref, cb_ref,
                      kcat_ref, v_ref, *, n_heads, rank):
    rows = h_ref.shape[0] // ROW_SPLITS
    for c in range(ROW_SPLITS):
        r = slice(c * rows, (c + 1) * rows)
        xn = _rms(h_ref[r, :], g_ref[...]).astype(BF16)
        ckv = jnp.dot(xn, wdkv_ref[...], preferred_element_type=F32)
        c_latent = _rms(ckv[:, :rank], gl_ref[...]).astype(BF16)
        k_rope = _rope(ckv[:, rank:], ca_ref[r, :], cb_ref[r, :]).astype(BF16)
        for h in range(n_heads):
            kv = jnp.dot(c_latent, wukv_ref[:, h * 256:(h + 1) * 256], preferred_element_type=F32)
            kcat_ref[r, h * MLA_QK_PAD:h * MLA_QK_PAD + MLA_NOPE_DIM] = kv[:, :MLA_NOPE_DIM].astype(BF16)
            kcat_ref[r, h * MLA_QK_PAD + MLA_NOPE_DIM:(h + 1) * MLA_QK_PAD] = k_rope
            v_ref[r, h * MLA_V_DIM:(h + 1) * MLA_V_DIM] = kv[:, MLA_NOPE_DIM:].astype(BF16)


def _latent_kv(h, g, w_dkv_pad, g_latent, w_ukv, ca, cb, n_heads, *, tm):
    m, d = h.shape
    rank = g_latent.shape[0]
    tm = _tile(m, tm)
    row = lambda i: (i, 0)
    fixed = lambda i: (0, 0)
    return pl.pallas_call(
        functools.partial(_latent_kv_kernel, n_heads=n_heads, rank=rank),
        out_shape=(jax.ShapeDtypeStruct((m, n_heads * MLA_QK_PAD), BF16),
                   jax.ShapeDtypeStruct((m, n_heads * MLA_V_DIM), BF16)),
        grid=(m // tm,),
        in_specs=[
            pl.BlockSpec((tm, d), row),
            pl.BlockSpec((1, d), fixed),
            pl.BlockSpec(w_dkv_pad.shape, fixed),
            pl.BlockSpec((1, rank), fixed),
            pl.BlockSpec(w_ukv.shape, fixed),
            pl.BlockSpec((tm, LANES), row),
            pl.BlockSpec((tm, LANES), row),
        ],
        out_specs=(pl.BlockSpec((tm, n_heads * MLA_QK_PAD), row),
                   pl.BlockSpec((tm, n_heads * MLA_V_DIM), row)),
        compiler_params=_params("parallel"),
        name="latent_kv",
    )(h, g.reshape(1, d), w_dkv_pad, g_latent.reshape(1, rank), w_ukv, ca, cb)


def _mla_in_kernel(h_ref, g_ref, win_ref, gq_ref, wuq_ref, ca_ref, cb_ref,
                   qcat_ref, memq_ref, *, n_heads, rank):
    rows = h_ref.shape[0] // ROW_SPLITS
    for c in range(ROW_SPLITS):
        r = slice(c * rows, (c + 1) * rows)
        xn = _rms(h_ref[r, :], g_ref[...]).astype(BF16)
        proj = jnp.dot(xn, win_ref[...], preferred_element_type=F32)
        memq_ref[r, :] = proj[:, rank:].astype(BF16)
        c_q = _rms(proj[:, :rank], gq_ref[...]).astype(BF16)
        ca, cb = ca_ref[r, :], cb_ref[r, :]
        for h in range(n_heads):
            q = jnp.dot(c_q, wuq_ref[:, h * MLA_QK_PAD:(h + 1) * MLA_QK_PAD], preferred_element_type=F32)
            qcat_ref[r, h * MLA_QK_PAD:h * MLA_QK_PAD + MLA_NOPE_DIM] = q[:, :MLA_NOPE_DIM].astype(BF16)
            qcat_ref[r, h * MLA_QK_PAD + MLA_NOPE_DIM:(h + 1) * MLA_QK_PAD] = (
                _rope(q[:, MLA_NOPE_DIM:], ca, cb).astype(BF16))


def _mla_in(h, g, w_in, g_q, w_uq_pad, layer, ca, cb, n_heads, *, tm):
    m, d = h.shape
    rank = g_q.shape[0]
    n_memq = w_in.shape[2] - rank
    tm = _tile(m, tm)
    row = lambda i: (i, 0)
    fixed = lambda i: (0, 0)
    stacked = lambda i: (layer, 0, 0)
    return pl.pallas_call(
        functools.partial(_mla_in_kernel, n_heads=n_heads, rank=rank),
        out_shape=(jax.ShapeDtypeStruct((m, n_heads * MLA_QK_PAD), BF16),
                   jax.ShapeDtypeStruct((m, n_memq), BF16)),
        grid=(m // tm,),
        in_specs=[
            pl.BlockSpec((tm, d), row),
            pl.BlockSpec((1, d), fixed),
            pl.BlockSpec((None,) + w_in.shape[1:], stacked),
            pl.BlockSpec((1, rank), fixed),
            pl.BlockSpec((None,) + w_uq_pad.shape[1:], stacked),
            pl.BlockSpec((tm, LANES), row),
            pl.BlockSpec((tm, LANES), row),
        ],
        out_specs=(pl.BlockSpec((tm, n_heads * MLA_QK_PAD), row),
                   pl.BlockSpec((tm, n_memq), row)),
        compiler_params=_params("parallel"),
        name="mla_in",
    )(h, g.reshape(1, d), w_in, g_q.reshape(1, rank), w_uq_pad, ca, cb)


def _rope_tables(positions):
    half = MLA_ROPE_DIM // 2
    inv_freq = ROPE_THETA ** (-jnp.arange(half, dtype=F32) / half)
    ang = positions.reshape(-1).astype(F32)[:, None] * inv_freq
    cos, sin = jnp.cos(ang), jnp.sin(ang)
    zeros = jnp.zeros((ang.shape[0], LANES - MLA_ROPE_DIM), F32)
    ca = jnp.concatenate([cos, cos, zeros], axis=-1)
    cb = jnp.concatenate([-sin, sin, zeros], axis=-1)
    return ca, cb


def kernel(x, mem, positions, attn_norm_g, ffn_norm_g, a_w_in, a_w_out, b_w_in, b_q_norm_g, b_w_uq, b_w_out, mem_norm_g, w_mem_kv, kv_norm_g, w_dkv, kv_latent_g, w_ukv, ffn_w_gu, ffn_w_down, final_norm_g):
    b, s, d = x.shape
    mem_len = mem.shape[1]
    depth = attn_norm_g.shape[0]
    n_a = a_w_in.shape[0]
    n_b = b_w_in.shape[0]
    mq_w = MEM_HEADS * HEAD_DIM
    sb_heads = (a_w_in.shape[2] - mq_w) // (3 * HEAD_DIM)
    q_rank = b_q_norm_g.shape[1]
    kv_rank = kv_latent_g.shape[0]
    mla_heads = b_w_uq.shape[2] // (MLA_NOPE_DIM + MLA_ROPE_DIM)

    a_w_in_b = a_w_in.astype(BF16)
    a_w_out_b = a_w_out.astype(BF16)
    b_w_in_b = b_w_in.astype(BF16)
    b_w_out_b = b_w_out.astype(BF16)
    w_ukv_b = w_ukv.astype(BF16)
    w_mem_all = jnp.transpose(w_mem_kv, (1, 0, 2)).reshape(d, depth * 2 * mq_w).astype(BF16)
    w_dkv_pad = jnp.pad(w_dkv, ((0, 0), (0, LANES - MLA_ROPE_DIM))).astype(BF16)
    w_uq_pad = jnp.pad(
        b_w_uq.reshape(n_b, q_rank, mla_heads, MLA_NOPE_DIM + MLA_ROPE_DIM),
        ((0, 0), (0, 0), (0, 0), (0, MLA_QK_PAD - MLA_NOPE_DIM - MLA_ROPE_DIM)),
    ).reshape(n_b, q_rank, mla_heads * MLA_QK_PAD).astype(BF16)
    ca, cb = _rope_tables(positions)
    a_col_scale = jnp.concatenate([jnp.full((sb_heads * HEAD_DIM,), SB_Q_PRESCALE, F32),
                                   jnp.ones((a_w_in.shape[2] - sb_heads * HEAD_DIM,), F32)])

    h = x.reshape(b * s, d)
    mkv = _norm_matmul(mem.reshape(b * mem_len, d), mem_norm_g, w_mem_all[None], 0,
                       jnp.ones((w_mem_all.shape[1],), F32), tm=NORM_MATMUL_ROWS, tn=NORM_MATMUL_COLS)
    mkv = mkv.reshape(b, mem_len, depth * 2 * mq_w)

    k_cat = v_lat = None
    for layer in range(depth):
        if layer == n_a:
            k_cat, v_lat = _latent_kv(h, kv_norm_g, w_dkv_pad, kv_latent_g, w_ukv_b, ca, cb,
                                      mla_heads, tm=LATENT_KV_ROWS)
            k_cat = k_cat.reshape(b, s, -1)
            v_lat = v_lat.reshape(b, s, -1)
        if layer < n_a:
            proj = _norm_matmul(h, attn_norm_g[layer], a_w_in_b, layer, a_col_scale,
                                tm=NORM_MATMUL_ROWS, tn=NORM_MATMUL_COLS)
            mix = _sb_attention(proj.reshape(b, s, -1), sb_heads, blk=ATTENTION_BLOCK)
            q_src, q_block = proj, (3 * sb_heads * HEAD_DIM) // mq_w
            w_out, li = a_w_out_b, layer
        else:
            li = layer - n_a
            q_cat, mem_q = _mla_in(h, attn_norm_g[layer], b_w_in_b, b_q_norm_g[li], w_uq_pad, li,
                                   ca, cb, mla_heads, tm=MLA_IN_ROWS)
            mix = _mla_attention(q_cat.reshape(b, s, -1), k_cat, v_lat, mla_heads, blk=ATTENTION_BLOCK)
            q_src, q_block = mem_q, 0
            w_out = b_w_out_b
        h, w_gu_b, w_d_b = _mem_out_proj(mix.reshape(b * s, -1), q_src, q_block, mkv, 2 * layer, w_out, li, h,
                                         ffn_w_gu, ffn_w_down, layer, tm=MEM_OUT_PROJ_ROWS)
        h = _ffn(h, ffn_norm_g[layer], w_gu_b[None], w_d_b[None], 0, final_norm_g,
                 final_norm=(layer == depth - 1), tm=FFN_ROWS, tf=FFN_HIDDEN)
    return h.reshape(b, s, d)
```

```python
import functools

import jax
import jax.numpy as jnp
from jax import lax
from jax.experimental import pallas as pl
from jax.experimental.pallas import tpu as pltpu

HEAD_DIM = 128
MEM_HEADS = 4
MLA_NOPE_DIM = 128
MLA_ROPE_DIM = 64
MLA_V_DIM = 128
ROPE_THETA = 10000.0
RMS_EPS = 1e-6
LOG2_E = 1.4426950408889634
SB_Q_PRESCALE = -(HEAD_DIM ** -0.5) * LOG2_E
SB_NEAR_KEY_BLOCKS = 2
SB_ZERO_WEIGHT_LOG2 = -140.0

LANES = 128
MLA_QK_PAD = 2 * LANES
NORM_CHUNK_ROWS = 256
FFN_FIRST_STEP_SPLITS = 2
NORM_MATMUL_FIRST_STEP_SPLITS = 2
ROW_SPLITS = 2

VMEM_LIMIT_BYTES = 48 * 1024 * 1024
VMEM_LIMIT_BYTES_LARGE = 58 * 1024 * 1024
NORM_MATMUL_ROWS, NORM_MATMUL_COLS = 512, 1024
FFN_ROWS, FFN_HIDDEN = 1024, 512
MEM_OUT_PROJ_ROWS = 512
MLA_IN_ROWS = 1024
LATENT_KV_ROWS = 1024
ATTENTION_BLOCK = 256
SB_HEADS_PER_STEP = 2
MLA_HEADS_PER_STEP = 4

BF16 = jnp.bfloat16
F32 = jnp.float32


def _params(*semantics, vmem_limit_bytes=VMEM_LIMIT_BYTES):
    return pltpu.CompilerParams(dimension_semantics=semantics, vmem_limit_bytes=vmem_limit_bytes)


def _tile(n, pref):
    t = min(n, pref)
    assert n % t == 0, (n, pref)
    return t


def _rms(x, g):
    return (x * lax.rsqrt(jnp.mean(x * x, axis=-1, keepdims=True) + RMS_EPS)) * g


def _rmsnorm_rows(x_ref, g_ref, out_ref):
    rows = x_ref.shape[0]
    chunk = min(rows, NORM_CHUNK_ROWS)
    g = g_ref[...]

    def body(c, carry):
        r = pl.multiple_of(c * chunk, chunk)
        out_ref[pl.ds(r, chunk), :] = _rms(x_ref[pl.ds(r, chunk), :], g).astype(out_ref.dtype)
        return carry

    lax.fori_loop(0, rows // chunk, body, 0)


def _rope(x, ca, cb):
    partner = pltpu.roll(x, 96, 1) + pltpu.roll(x, 32, 1)
    return x * ca + partner * cb


def _norm_matmul_kernel(h_ref, g_ref, w_ref, cs_ref, o_ref, xn_ref, *, tn):
    def column_tile(xn, j):
        cols = slice(j * tn, (j + 1) * tn)
        return (jnp.dot(xn, w_ref[:, cols], preferred_element_type=F32) * cs_ref[:, cols]).astype(o_ref.dtype)

    rows = h_ref.shape[0] // NORM_MATMUL_FIRST_STEP_SPLITS
    g = g_ref[...]
    for c in range(NORM_MATMUL_FIRST_STEP_SPLITS):
        r = slice(c * rows, (c + 1) * rows)
        xn = _rms(h_ref[r, :], g).astype(BF16)
        xn_ref[r, :] = xn
        o_ref[r, 0:tn] = column_tile(xn, 0)
    for j in range(1, o_ref.shape[1] // tn):
        o_ref[:, j * tn:(j + 1) * tn] = column_tile(xn_ref[...], j)


def _norm_matmul(h, g, w, layer, col_scale, *, tm, tn):
    m, d = h.shape
    n = w.shape[2]
    tm, tn = _tile(m, tm), _tile(n, tn)
    return pl.pallas_call(
        functools.partial(_norm_matmul_kernel, tn=tn),
        out_shape=jax.ShapeDtypeStruct((m, n), BF16),
        grid=(m // tm,),
        in_specs=[
            pl.BlockSpec((tm, d), lambda i: (i, 0)),
            pl.BlockSpec((1, d), lambda i: (0, 0)),
            pl.BlockSpec((None, d, n), lambda i: (layer, 0, 0), pipeline_mode=pl.Buffered(1)),
            pl.BlockSpec((1, n), lambda i: (0, 0)),
        ],
        out_specs=pl.BlockSpec((tm, n), lambda i: (i, 0)),
        scratch_shapes=[pltpu.VMEM((tm, d), BF16)],
        compiler_params=_params("parallel", vmem_limit_bytes=VMEM_LIMIT_BYTES_LARGE),
        name="norm_matmul",
    )(h, g.reshape(1, d), w, col_scale.reshape(1, n))


def _ffn_kernel(h_ref, g_ref, wg_ref, wu_ref, wd_ref, gf_ref, o_ref, xn_ref, *, final_norm):
    j = pl.program_id(1)

    def hidden_tile(xn):
        gate = jnp.dot(xn, wg_ref[...], preferred_element_type=F32)
        up = jnp.dot(xn, wu_ref[...], preferred_element_type=F32)
        act = (gate * jax.nn.sigmoid(gate) * up).astype(BF16)
        return jnp.dot(act, wd_ref[...], preferred_element_type=F32)

    @pl.when(j == 0)
    def _():
        half = h_ref.shape[0] // FFN_FIRST_STEP_SPLITS
        g = g_ref[...]
        for c in range(FFN_FIRST_STEP_SPLITS):
            h = h_ref[c * half:(c + 1) * half, :]
            xn = _rms(h, g).astype(BF16)
            xn_ref[c * half:(c + 1) * half, :] = xn
            o_ref[c * half:(c + 1) * half, :] = h + hidden_tile(xn)

    @pl.when(j > 0)
    def _():
        o_ref[...] += hidden_tile(xn_ref[...])

    if final_norm:
        @pl.when(j == pl.num_programs(1) - 1)
        def _():
            _rmsnorm_rows(o_ref, gf_ref, o_ref)


def _ffn(h, g, w_gu, w_d, layer, gf, *, final_norm, tm, tf):
    m, d = h.shape
    f = w_d.shape[1]
    tm, tf = _tile(m, tm), _tile(f, tf)
    nf = f // tf
    return pl.pallas_call(
        functools.partial(_ffn_kernel, final_norm=final_norm),
        out_shape=jax.ShapeDtypeStruct((m, d), F32),
        grid=(m // tm, nf),
        in_specs=[
            pl.BlockSpec((tm, d), lambda i, j: (i, 0)),
            pl.BlockSpec((1, d), lambda i, j: (0, 0)),
            pl.BlockSpec((None, d, tf), lambda i, j: (layer, 0, j)),
            pl.BlockSpec((None, d, tf), lambda i, j: (layer, 0, nf + j)),
            pl.BlockSpec((None, tf, d), lambda i, j: (layer, j, 0)),
            pl.BlockSpec((1, d), lambda i, j: (0, 0)),
        ],
        out_specs=pl.BlockSpec((tm, d), lambda i, j: (i, 0)),
        scratch_shapes=[pltpu.VMEM((tm, d), BF16)],
        compiler_params=_params("parallel", "arbitrary", vmem_limit_bytes=VMEM_LIMIT_BYTES_LARGE),
        name="ffn",
    )(h, g.reshape(1, d), w_gu, w_gu, w_d, gf.reshape(1, d))


def _mem_out_proj_kernel(mix_ref, qm_ref, mk_ref, mv_ref, w1_ref, w2_ref, h_ref, gu_ref, wd_ref,
                         o_ref, gu_b_ref, wd_b_ref, *, scale):
    gu_b_ref[...] = gu_ref[...].astype(BF16)
    wd_b_ref[...] = wd_ref[...].astype(BF16)
    acc = h_ref[...] + jnp.dot(mix_ref[...], w1_ref[...], preferred_element_type=F32)
    heads = []
    for hd in range(MEM_HEADS):
        cols = slice(hd * HEAD_DIM, (hd + 1) * HEAD_DIM)
        s = lax.dot_general(qm_ref[:, cols], mk_ref[:, cols], (((1,), (1,)), ((), ())),
                            preferred_element_type=F32)
        m = jnp.max(s, axis=1, keepdims=True)
        p = jnp.exp2((s - m) * (scale * LOG2_E))
        l = jnp.sum(p, axis=1, keepdims=True)
        o = jnp.dot(p.astype(BF16), mv_ref[:, cols], preferred_element_type=F32)
        heads.append((o / l).astype(BF16))
    mem_out = jnp.concatenate(heads, axis=1)
    o_ref[...] = acc + jnp.dot(mem_out, w2_ref[...], preferred_element_type=F32)


def _cast_rows(total_rows, steps):
    for rows in range(16, total_rows + 1, 16):
        if total_rows % rows == 0 and total_rows // rows <= steps:
            return rows
    raise ValueError((total_rows, steps))


def _mem_out_proj(mix, q_src, q_block, mkv, kv_block, w_out, layer, h, ffn_w_gu, ffn_w_d, ffn_layer, *, tm):
    m, d = h.shape
    bsz, ml, _ = mkv.shape
    k1, k2 = mix.shape[1], MEM_HEADS * HEAD_DIM
    assert k1 % k2 == 0 and w_out.shape[1] == k1 + k2
    tm = _tile(m // bsz, tm)
    steps_per_batch = (m // bsz) // tm
    steps = m // tm
    gu_rows, wd_rows = _cast_rows(ffn_w_gu.shape[1], steps), _cast_rows(ffn_w_d.shape[1], steps)
    gu_last, wd_last = ffn_w_gu.shape[1] // gu_rows - 1, ffn_w_d.shape[1] // wd_rows - 1
    return pl.pallas_call(
        functools.partial(_mem_out_proj_kernel, scale=HEAD_DIM ** -0.5),
        out_shape=(jax.ShapeDtypeStruct((m, d), F32),
                   jax.ShapeDtypeStruct(ffn_w_gu.shape[1:], BF16),
                   jax.ShapeDtypeStruct(ffn_w_d.shape[1:], BF16)),
        grid=(steps,),
        in_specs=[
            pl.BlockSpec((tm, k1), lambda i: (i, 0)),
            pl.BlockSpec((tm, k2), lambda i: (i, q_block)),
            pl.BlockSpec((None, ml, k2), lambda i: (i // steps_per_batch, 0, kv_block)),
            pl.BlockSpec((None, ml, k2), lambda i: (i // steps_per_batch, 0, kv_block + 1)),
            pl.BlockSpec((None, k1, d), lambda i: (layer, 0, 0)),
            pl.BlockSpec((None, k2, d), lambda i: (layer, k1 // k2, 0)),
            pl.BlockSpec((tm, d), lambda i: (i, 0)),
            pl.BlockSpec((None, gu_rows, ffn_w_gu.shape[2]), lambda i: (ffn_layer, jnp.minimum(i, gu_last), 0)),
            pl.BlockSpec((None, wd_rows, ffn_w_d.shape[2]), lambda i: (ffn_layer, jnp.minimum(i, wd_last), 0)),
        ],
        out_specs=(pl.BlockSpec((tm, d), lambda i: (i, 0)),
                   pl.BlockSpec((gu_rows, ffn_w_gu.shape[2]), lambda i: (jnp.minimum(i, gu_last), 0)),
                   pl.BlockSpec((wd_rows, ffn_w_d.shape[2]), lambda i: (jnp.minimum(i, wd_last), 0))),
        compiler_params=_params("arbitrary"),
        name="mem_out_proj",
    )(mix, q_src, mkv, mkv, w_out, w_out, h, ffn_w_gu, ffn_w_d)


def _sb_attn_kernel(q_ref, k_ref, v_ref, tri_ref, o_ref, acc_ref, carry_ref, *, blk):
    for hd in range(q_ref.shape[1] // HEAD_DIM):
        cols = slice(hd * HEAD_DIM, (hd + 1) * HEAD_DIM)
        _sb_attn_head(q_ref.at[:, cols], k_ref.at[:, cols], v_ref.at[:, cols], tri_ref, o_ref.at[:, cols],
                      acc_ref, carry_ref, blk=blk)


def _sb_attn_head(q_ref, k_ref, v_ref, tri_ref, o_ref, acc_ref, carry_ref, *, blk):
    tri2 = tri_ref[...]
    row = lax.broadcasted_iota(jnp.int32, (blk, blk), 0)
    col = lax.broadcasted_iota(jnp.int32, (blk, blk), 1)
    strictly_past = col < row
    nq = q_ref.shape[0] // blk

    def rows(i):
        return slice(i * blk, (i + 1) * blk)

    def log_keep(nz):
        neg_abs = lax.bitcast_convert_type(
            lax.bitcast_convert_type(nz, jnp.uint32) | jnp.uint32(0x80000000), F32)
        return jnp.minimum(nz, 0.0) - jnp.log2(1.0 + jnp.exp2(neg_abs))

    def block_weights(nz, lk, carry, diagonal):
        if diagonal:
            lk = jnp.where(strictly_past, lk, 0.0)
        hi = lk.astype(BF16)
        lo = (lk - hi.astype(F32)).astype(BF16)
        log_w = jnp.dot(jnp.concatenate([hi, lo], axis=1), tri2, preferred_element_type=F32) - nz
        if carry is not None:
            log_w = log_w + carry
        w = jnp.exp2(log_w)
        if diagonal:
            w = jnp.where(strictly_past, w, 0.0)
        block_sum = jnp.sum(lk, axis=1, keepdims=True)
        return w.astype(BF16), (block_sum if carry is None else carry + block_sum)

    def first_key_block(i):
        return max(i - SB_NEAR_KEY_BLOCKS + 1, 0)

    def logits(i):
        keys = slice(first_key_block(i) * blk, (i + 1) * blk)
        return lax.dot_general(q_ref[rows(i), :], k_ref[keys, :], (((1,), (1,)), ((), ())),
                               preferred_element_type=F32)

    def finish(i, nz, lkeep):
        j0 = first_key_block(i)
        w_blocks, carry = [None] * (i + 1 - j0), None
        for j in range(i, j0 - 1, -1):
            cols = slice((j - j0) * blk, (j - j0 + 1) * blk)
            w_blocks[j - j0], carry = block_weights(nz[:, cols], lkeep[:, cols], carry, j == i)
        w = w_blocks[0] if len(w_blocks) == 1 else jnp.concatenate(w_blocks, axis=1)
        acc = jnp.dot(w, v_ref[j0 * blk:(i + 1) * blk, :], preferred_element_type=F32)
        o_ref[rows(i), :] = acc.astype(o_ref.dtype)
        if j0 > 0:
            acc_ref[rows(i), :] = acc
            carry_ref[rows(i), :] = jnp.broadcast_to(carry, (blk, LANES))
            q32 = q_ref[rows(i), :].astype(F32)
            logit_bound[i] = jnp.sqrt(jnp.max(jnp.sum(q32 * q32, axis=1, keepdims=True)) * k_sq) * 1.01 + 1.0
            carry_max[i] = jnp.max(carry)

    if nq > SB_NEAR_KEY_BLOCKS:
        k32 = k_ref[...].astype(F32)
        k_sq = jnp.max(jnp.sum(k32 * k32, axis=1, keepdims=True))
    logit_bound, carry_max = {}, {}

    order = list(range(nq - 1, -1, -1))
    nz_of, lk_of = {}, {}
    for step in range(nq + 2):
        if step < nq:
            nz_of[order[step]] = logits(order[step])
        if 0 <= step - 1 < nq:
            lk_of[order[step - 1]] = log_keep(nz_of[order[step - 1]])
        if 0 <= step - 2 < nq:
            i = order[step - 2]
            finish(i, nz_of.pop(i), lk_of.pop(i))

    for i in range(SB_NEAR_KEY_BLOCKS, nq):
        def more(state, i=i):
            j, c_max = state
            return jnp.logical_and(j >= 0, c_max + logit_bound[i] >= SB_ZERO_WEIGHT_LOG2)

        def take_block(state, i=i):
            j, _ = state
            keys = pl.ds(pl.multiple_of(j * blk, blk), blk)
            nz = lax.dot_general(q_ref[rows(i), :], k_ref[keys, :], (((1,), (1,)), ((), ())),
                                 preferred_element_type=F32)
            w, carry = block_weights(nz, log_keep(nz), carry_ref[rows(i), 0:1], False)
            acc_ref[rows(i), :] += jnp.dot(w, v_ref[keys, :], preferred_element_type=F32)
            carry_ref[rows(i), :] = jnp.broadcast_to(carry, (blk, LANES))
            return j - 1, jnp.max(carry)

        j_end, _ = lax.while_loop(more, take_block, (jnp.int32(i - SB_NEAR_KEY_BLOCKS), carry_max[i]))

        @pl.when(j_end < i - SB_NEAR_KEY_BLOCKS)
        def _(i=i):
            o_ref[rows(i), :] = acc_ref[rows(i), :].astype(o_ref.dtype)


def _sb_attention(proj, n_heads, *, blk, heads_per_step):
    b, s, _ = proj.shape
    blk = _tile(s, blk)
    tri = (jnp.arange(blk)[:, None] >= jnp.arange(blk)[None, :]).astype(BF16)
    tri2 = jnp.concatenate([tri, tri], axis=0)
    groups, width = n_heads // _tile(n_heads, heads_per_step), _tile(n_heads, heads_per_step) * HEAD_DIM
    return pl.pallas_call(
        functools.partial(_sb_attn_kernel, blk=blk),
        out_shape=jax.ShapeDtypeStruct((b, s, n_heads * HEAD_DIM), BF16),
        grid=(b, groups),
        in_specs=[
            pl.BlockSpec((None, s, width), lambda bi, h: (bi, 0, h)),
            pl.BlockSpec((None, s, width), lambda bi, h: (bi, 0, groups + h)),
            pl.BlockSpec((None, s, width), lambda bi, h: (bi, 0, 2 * groups + h)),
            pl.BlockSpec((2 * blk, blk), lambda bi, h: (0, 0)),
        ],
        out_specs=pl.BlockSpec((None, s, width), lambda bi, h: (bi, 0, h)),
        scratch_shapes=[pltpu.VMEM((s, HEAD_DIM), F32), pltpu.VMEM((s, LANES), F32)],
        compiler_params=_params("parallel", "parallel"),
        name="sb_attention",
    )(proj, proj, proj, tri2)


def _mla_attn_kernel(q_ref, k_ref, v_ref, o_ref, *, blk, scale):
    for hd in range(v_ref.shape[1] // MLA_V_DIM):
        qk_cols = slice(hd * MLA_QK_PAD, (hd + 1) * MLA_QK_PAD)
        v_cols = slice(hd * MLA_V_DIM, (hd + 1) * MLA_V_DIM)
        _mla_attn_head(q_ref.at[:, qk_cols], k_ref.at[:, qk_cols], v_ref.at[:, v_cols], o_ref.at[:, v_cols],
                       blk=blk, scale=scale)


def _mla_attn_head(q_ref, k_ref, v_ref, o_ref, *, blk, scale):
    row = lax.broadcasted_iota(jnp.int32, (blk, blk), 0)
    col = lax.broadcasted_iota(jnp.int32, (blk, blk), 1)
    causal = col <= row
    nq = q_ref.shape[0] // blk

    def scores(i):
        q = q_ref[i * blk:(i + 1) * blk, :]
        return lax.dot_general(q, k_ref[0:(i + 1) * blk, :], (((1,), (1,)), ((), ())),
                               preferred_element_type=F32)

    def probs(i, s):
        diag = jnp.where(causal, s[:, i * blk:], -jnp.inf)
        s = diag if i == 0 else jnp.concatenate([s[:, :i * blk], diag], axis=1)
        m = jnp.max(s, axis=1, keepdims=True)
        p = jnp.exp2((s - m) * (scale * LOG2_E))
        return p.astype(BF16), jnp.sum(p, axis=1, keepdims=True)

    def finish(i, p, l):
        o = jnp.dot(p, v_ref[0:(i + 1) * blk, :], preferred_element_type=F32)
        o_ref[i * blk:(i + 1) * blk, :] = (o / l).astype(o_ref.dtype)

    order = list(range(nq - 1, -1, -1))
    s_of, p_of = {}, {}
    for step in range(nq + 2):
        if step < nq:
            s_of[order[step]] = scores(order[step])
        if 0 <= step - 1 < nq:
            i = order[step - 1]
            p_of[i] = probs(i, s_of.pop(i))
        if 0 <= step - 2 < nq:
            i = order[step - 2]
            finish(i, *p_of.pop(i))


def _mla_attention(q_cat, k_cat, v, n_heads, *, blk, heads_per_step):
    b, s, _ = q_cat.shape
    blk = _tile(s, blk)
    scale = (MLA_NOPE_DIM + MLA_ROPE_DIM) ** -0.5
    per_step = _tile(n_heads, heads_per_step)
    return pl.pallas_call(
        functools.partial(_mla_attn_kernel, blk=blk, scale=scale),
        out_shape=jax.ShapeDtypeStruct((b, s, n_heads * MLA_V_DIM), BF16),
        grid=(b, n_heads // per_step),
        in_specs=[
            pl.BlockSpec((None, s, per_step * MLA_QK_PAD), lambda bi, h: (bi, 0, h)),
            pl.BlockSpec((None, s, per_step * MLA_QK_PAD), lambda bi, h: (bi, 0, h)),
            pl.BlockSpec((None, s, per_step * MLA_V_DIM), lambda bi, h: (bi, 0, h)),
        ],
        out_specs=pl.BlockSpec((None, s, per_step * MLA_V_DIM), lambda bi, h: (bi, 0, h)),
        compiler_params=_params("parallel", "parallel"),
        name="mla_attention",
    )(q_cat, k_cat, v)


def _latent_kv_kernel(h_ref, g_ref, wdkv_ref, gl_ref, wukv_ref, ca_ref, cb_ref,
                      kcat_ref, v_ref, *, n_heads, rank):
    rows = h_ref.shape[0] // ROW_SPLITS
    for c in range(ROW_SPLITS):
        r = slice(c * rows, (c + 1) * rows)
        xn = _rms(h_ref[r, :], g_ref[...]).astype(BF16)
        ckv = jnp.dot(xn, wdkv_ref[...], preferred_element_type=F32)
        c_latent = _rms(ckv[:, :rank], gl_ref[...]).astype(BF16)
        k_rope = _rope(ckv[:, rank:], ca_ref[r, :], cb_ref[r, :]).astype(BF16)
        for h in range(n_heads):
            kv = jnp.dot(c_latent, wukv_ref[:, h * 256:(h + 1) * 256], preferred_element_type=F32)
            kcat_ref[r, h * MLA_QK_PAD:h * MLA_QK_PAD + MLA_NOPE_DIM] = kv[:, :MLA_NOPE_DIM].astype(BF16)
            kcat_ref[r, h * MLA_QK_PAD + MLA_NOPE_DIM:(h + 1) * MLA_QK_PAD] = k_rope
            v_ref[r, h * MLA_V_DIM:(h + 1) * MLA_V_DIM] = kv[:, MLA_NOPE_DIM:].astype(BF16)


def _latent_kv(h, g, w_dkv_pad, g_latent, w_ukv, ca, cb, n_heads, *, tm):
    m, d = h.shape
    rank = g_latent.shape[0]
    tm = _tile(m, tm)
    row = lambda i: (i, 0)
    fixed = lambda i: (0, 0)
    return pl.pallas_call(
        functools.partial(_latent_kv_kernel, n_heads=n_heads, rank=rank),
        out_shape=(jax.ShapeDtypeStruct((m, n_heads * MLA_QK_PAD), BF16),
                   jax.ShapeDtypeStruct((m, n_heads * MLA_V_DIM), BF16)),
        grid=(m // tm,),
        in_specs=[
            pl.BlockSpec((tm, d), row),
            pl.BlockSpec((1, d), fixed),
            pl.BlockSpec(w_dkv_pad.shape, fixed),
            pl.BlockSpec((1, rank), fixed),
            pl.BlockSpec(w_ukv.shape, fixed),
            pl.BlockSpec((tm, LANES), row),
            pl.BlockSpec((tm, LANES), row),
        ],
        out_specs=(pl.BlockSpec((tm, n_heads * MLA_QK_PAD), row),
                   pl.BlockSpec((tm, n_heads * MLA_V_DIM), row)),
        compiler_params=_params("parallel"),
        name="latent_kv",
    )(h, g.reshape(1, d), w_dkv_pad, g_latent.reshape(1, rank), w_ukv, ca, cb)


def _mla_in_kernel(h_ref, g_ref, win_ref, gq_ref, wuq_ref, ca_ref, cb_ref,
                   qcat_ref, memq_ref, *, n_heads, rank):
    rows = h_ref.shape[0] // ROW_SPLITS
    for c in range(ROW_SPLITS):
        r = slice(c * rows, (c + 1) * rows)
        xn = _rms(h_ref[r, :], g_ref[...]).astype(BF16)
        proj = jnp.dot(xn, win_ref[...], preferred_element_type=F32)
        memq_ref[r, :] = proj[:, rank:].astype(BF16)
        c_q = _rms(proj[:, :rank], gq_ref[...]).astype(BF16)
        ca, cb = ca_ref[r, :], cb_ref[r, :]
        for h in range(n_heads):
            q = jnp.dot(c_q, wuq_ref[:, h * MLA_QK_PAD:(h + 1) * MLA_QK_PAD], preferred_element_type=F32)
            qcat_ref[r, h * MLA_QK_PAD:h * MLA_QK_PAD + MLA_NOPE_DIM] = q[:, :MLA_NOPE_DIM].astype(BF16)
            qcat_ref[r, h * MLA_QK_PAD + MLA_NOPE_DIM:(h + 1) * MLA_QK_PAD] = (
                _rope(q[:, MLA_NOPE_DIM:], ca, cb).astype(BF16))


def _mla_in(h, g, w_in, g_q, w_uq_pad, layer, ca, cb, n_heads, *, tm):
    m, d = h.shape
    rank = g_q.shape[0]
    n_memq = w_in.shape[2] - rank
    tm = _tile(m, tm)
    row = lambda i: (i, 0)
    fixed = lambda i: (0, 0)
    stacked = lambda i: (layer, 0, 0)
    return pl.pallas_call(
        functools.partial(_mla_in_kernel, n_heads=n_heads, rank=rank),
        out_shape=(jax.ShapeDtypeStruct((m, n_heads * MLA_QK_PAD), BF16),
                   jax.ShapeDtypeStruct((m, n_memq), BF16)),
        grid=(m // tm,),
        in_specs=[
            pl.BlockSpec((tm, d), row),
            pl.BlockSpec((1, d), fixed),
            pl.BlockSpec((None,) + w_in.shape[1:], stacked),
            pl.BlockSpec((1, rank), fixed),
            pl.BlockSpec((None,) + w_uq_pad.shape[1:], stacked),
            pl.BlockSpec((tm, LANES), row),
            pl.BlockSpec((tm, LANES), row),
        ],
        out_specs=(pl.BlockSpec((tm, n_heads * MLA_QK_PAD), row),
                   pl.BlockSpec((tm, n_memq), row)),
        compiler_params=_params("parallel"),
        name="mla_in",
    )(h, g.reshape(1, d), w_in, g_q.reshape(1, rank), w_uq_pad, ca, cb)


def _rope_tables(positions):
    half = MLA_ROPE_DIM // 2
    inv_freq = ROPE_THETA ** (-jnp.arange(half, dtype=F32) / half)
    ang = positions.reshape(-1).astype(F32)[:, None] * inv_freq
    cos, sin = jnp.cos(ang), jnp.sin(ang)
    zeros = jnp.zeros((ang.shape[0], LANES - MLA_ROPE_DIM), F32)
    ca = jnp.concatenate([cos, cos, zeros], axis=-1)
    cb = jnp.concatenate([-sin, sin, zeros], axis=-1)
    return ca, cb


def kernel(x, mem, positions, attn_norm_g, ffn_norm_g, a_w_in, a_w_out, b_w_in, b_q_norm_g, b_w_uq, b_w_out, mem_norm_g, w_mem_kv, kv_norm_g, w_dkv, kv_latent_g, w_ukv, ffn_w_gu, ffn_w_down, final_norm_g):
    b, s, d = x.shape
    mem_len = mem.shape[1]
    depth = attn_norm_g.shape[0]
    n_a = a_w_in.shape[0]
    n_b = b_w_in.shape[0]
    mq_w = MEM_HEADS * HEAD_DIM
    sb_heads = (a_w_in.shape[2] - mq_w) // (3 * HEAD_DIM)
    q_rank = b_q_norm_g.shape[1]
    kv_rank = kv_latent_g.shape[0]
    mla_heads = b_w_uq.shape[2] // (MLA_NOPE_DIM + MLA_ROPE_DIM)

    a_w_in_b = a_w_in.astype(BF16)
    a_w_out_b = a_w_out.astype(BF16)
    b_w_in_b = b_w_in.astype(BF16)
    b_w_out_b = b_w_out.astype(BF16)
    w_ukv_b = w_ukv.astype(BF16)
    w_mem_all = jnp.transpose(w_mem_kv, (1, 0, 2)).reshape(d, depth * 2 * mq_w).astype(BF16)
    w_dkv_pad = jnp.pad(w_dkv, ((0, 0), (0, LANES - MLA_ROPE_DIM))).astype(BF16)
    w_uq_pad = jnp.pad(
        b_w_uq.reshape(n_b, q_rank, mla_heads, MLA_NOPE_DIM + MLA_ROPE_DIM),
        ((0, 0), (0, 0), (0, 0), (0, MLA_QK_PAD - MLA_NOPE_DIM - MLA_ROPE_DIM)),
    ).reshape(n_b, q_rank, mla_heads * MLA_QK_PAD).astype(BF16)
    ca, cb = _rope_tables(positions)
    a_col_scale = jnp.concatenate([jnp.full((sb_heads * HEAD_DIM,), SB_Q_PRESCALE, F32),
                                   jnp.ones((a_w_in.shape[2] - sb_heads * HEAD_DIM,), F32)])

    h = x.reshape(b * s, d)
    mkv = _norm_matmul(mem.reshape(b * mem_len, d), mem_norm_g, w_mem_all[None], 0,
                       jnp.ones((w_mem_all.shape[1],), F32), tm=NORM_MATMUL_ROWS, tn=NORM_MATMUL_COLS)
    mkv = mkv.reshape(b, mem_len, depth * 2 * mq_w)

    k_cat = v_lat = None
    for layer in range(depth):
        if layer == n_a:
            k_cat, v_lat = _latent_kv(h, kv_norm_g, w_dkv_pad, kv_latent_g, w_ukv_b, ca, cb,
                                      mla_heads, tm=LATENT_KV_ROWS)
            k_cat = k_cat.reshape(b, s, -1)
            v_lat = v_lat.reshape(b, s, -1)
        if layer < n_a:
            proj = _norm_matmul(h, attn_norm_g[layer], a_w_in_b, layer, a_col_scale,
                                tm=NORM_MATMUL_ROWS, tn=NORM_MATMUL_COLS)
            mix = _sb_attention(proj.reshape(b, s, -1), sb_heads, blk=ATTENTION_BLOCK,
                                heads_per_step=SB_HEADS_PER_STEP)
            q_src, q_block = proj, (3 * sb_heads * HEAD_DIM) // mq_w
            w_out, li = a_w_out_b, layer
        else:
            li = layer - n_a
            q_cat, mem_q = _mla_in(h, attn_norm_g[layer], b_w_in_b, b_q_norm_g[li], w_uq_pad, li,
                                   ca, cb, mla_heads, tm=MLA_IN_ROWS)
            mix = _mla_attention(q_cat.reshape(b, s, -1), k_cat, v_lat, mla_heads, blk=ATTENTION_BLOCK,
                                 heads_per_step=MLA_HEADS_PER_STEP)
            q_src, q_block = mem_q, 0
            w_out = b_w_out_b
        h, w_gu_b, w_d_b = _mem_out_proj(mix.reshape(b * s, -1), q_src, q_block, mkv, 2 * layer, w_out, li, h,
                                         ffn_w_gu, ffn_w_down, layer, tm=MEM_OUT_PROJ_ROWS)
        h = _ffn(h, ffn_norm_g[layer], w_gu_b[None], w_d_b[None], 0, final_norm_g,
                 final_norm=(layer == depth - 1), tm=FFN_ROWS, tf=FFN_HIDDEN)
    return h.reshape(b, s, d)
```

```python
import functools

import jax
import jax.numpy as jnp
from jax import lax
from jax.experimental import pallas as pl
from jax.experimental.pallas import tpu as pltpu

HEAD_DIM = 128
MEM_HEADS = 4
MLA_NOPE_DIM = 128
MLA_ROPE_DIM = 64
MLA_V_DIM = 128
ROPE_THETA = 10000.0
RMS_EPS = 1e-6
LOG2_E = 1.4426950408889634
SB_Q_PRESCALE = -(HEAD_DIM ** -0.5) * LOG2_E
SB_NEAR_KEY_BLOCKS = 2
SB_ZERO_WEIGHT_LOG2 = -140.0

LANES = 128
MLA_QK_PAD = 2 * LANES
NORM_CHUNK_ROWS = 256
FFN_FIRST_STEP_SPLITS = 2
NORM_MATMUL_FIRST_STEP_SPLITS = 2
ROW_SPLITS = 2

VMEM_LIMIT_BYTES = 48 * 1024 * 1024
VMEM_LIMIT_BYTES_LARGE = 58 * 1024 * 1024
NORM_MATMUL_ROWS, NORM_MATMUL_COLS = 512, 1024
FFN_ROWS, FFN_HIDDEN = 1024, 512
MEM_OUT_PROJ_ROWS = 512
MLA_IN_ROWS = 1024
LATENT_KV_ROWS = 1024
ATTENTION_BLOCK = 256
SB_HEADS_PER_STEP = 2
MLA_HEADS_PER_STEP = 4

BF16 = jnp.bfloat16
F32 = jnp.float32


def _params(*semantics, vmem_limit_bytes=VMEM_LIMIT_BYTES):
    return pltpu.CompilerParams(dimension_semantics=semantics, vmem_limit_bytes=vmem_limit_bytes)


def _tile(n, pref):
    t = min(n, pref)
    assert n % t == 0, (n, pref)
    return t


def _rms(x, g):
    return (x * lax.rsqrt(jnp.mean(x * x, axis=-1, keepdims=True) + RMS_EPS)) * g


def _rmsnorm_rows(x_ref, g_ref, out_ref):
    rows = x_ref.shape[0]
    chunk = min(rows, NORM_CHUNK_ROWS)
    g = g_ref[...]

    def body(c, carry):
        r = pl.multiple_of(c * chunk, chunk)
        out_ref[pl.ds(r, chunk), :] = _rms(x_ref[pl.ds(r, chunk), :], g).astype(out_ref.dtype)
        return carry

    lax.fori_loop(0, rows // chunk, body, 0)


def _rope(x, ca, cb):
    partner = pltpu.roll(x, 96, 1) + pltpu.roll(x, 32, 1)
    return x * ca + partner * cb


def _norm_matmul_kernel(h_ref, g_ref, w_ref, cs_ref, o_ref, xn_ref, *, tn):
    def column_tile(xn, j):
        cols = slice(j * tn, (j + 1) * tn)
        return (jnp.dot(xn, w_ref[:, cols], preferred_element_type=F32) * cs_ref[:, cols]).astype(o_ref.dtype)

    rows = h_ref.shape[0] // NORM_MATMUL_FIRST_STEP_SPLITS
    g = g_ref[...]
    for c in range(NORM_MATMUL_FIRST_STEP_SPLITS):
        r = slice(c * rows, (c + 1) * rows)
        xn = _rms(h_ref[r, :], g).astype(BF16)
        xn_ref[r, :] = xn
        o_ref[r, 0:tn] = column_tile(xn, 0)
    for j in range(1, o_ref.shape[1] // tn):
        o_ref[:, j * tn:(j + 1) * tn] = column_tile(xn_ref[...], j)


def _norm_matmul(h, g, w, layer, col_scale, *, tm, tn):
    m, d = h.shape
    n = w.shape[2]
    tm, tn = _tile(m, tm), _tile(n, tn)
    return pl.pallas_call(
        functools.partial(_norm_matmul_kernel, tn=tn),
        out_shape=jax.ShapeDtypeStruct((m, n), BF16),
        grid=(m // tm,),
        in_specs=[
            pl.BlockSpec((tm, d), lambda i: (i, 0)),
            pl.BlockSpec((1, d), lambda i: (0, 0)),
            pl.BlockSpec((None, d, n), lambda i: (layer, 0, 0), pipeline_mode=pl.Buffered(1)),
            pl.BlockSpec((1, n), lambda i: (0, 0)),
        ],
        out_specs=pl.BlockSpec((tm, n), lambda i: (i, 0)),
        scratch_shapes=[pltpu.VMEM((tm, d), BF16)],
        compiler_params=_params("parallel", vmem_limit_bytes=VMEM_LIMIT_BYTES_LARGE),
        name="norm_matmul",
    )(h, g.reshape(1, d), w, col_scale.reshape(1, n))


def _ffn_kernel(h_ref, g_ref, wg_ref, wu_ref, wd_ref, gf_ref, o_ref, xn_ref, *, final_norm):
    j = pl.program_id(1)

    def hidden_tile(xn):
        gate = jnp.dot(xn, wg_ref[...], preferred_element_type=F32)
        up = jnp.dot(xn, wu_ref[...], preferred_element_type=F32)
        act = (gate * jax.nn.sigmoid(gate) * up).astype(BF16)
        return jnp.dot(act, wd_ref[...], preferred_element_type=F32)

    @pl.when(j == 0)
    def _():
        half = h_ref.shape[0] // FFN_FIRST_STEP_SPLITS
        g = g_ref[...]
        for c in range(FFN_FIRST_STEP_SPLITS):
            h = h_ref[c * half:(c + 1) * half, :]
            xn = _rms(h, g).astype(BF16)
            xn_ref[c * half:(c + 1) * half, :] = xn
            o_ref[c * half:(c + 1) * half, :] = h + hidden_tile(xn)

    @pl.when(j > 0)
    def _():
        o_ref[...] += hidden_tile(xn_ref[...])

    if final_norm:
        @pl.when(j == pl.num_programs(1) - 1)
        def _():
            _rmsnorm_rows(o_ref, gf_ref, o_ref)


def _ffn(h, g, w_gu, w_d, layer, gf, *, final_norm, tm, tf):
    m, d = h.shape
    f = w_d.shape[1]
    tm, tf = _tile(m, tm), _tile(f, tf)
    nf = f // tf
    return pl.pallas_call(
        functools.partial(_ffn_kernel, final_norm=final_norm),
        out_shape=jax.ShapeDtypeStruct((m, d), F32),
        grid=(m // tm, nf),
        in_specs=[
            pl.BlockSpec((tm, d), lambda i, j: (i, 0)),
            pl.BlockSpec((1, d), lambda i, j: (0, 0)),
            pl.BlockSpec((None, d, tf), lambda i, j: (layer, 0, j)),
            pl.BlockSpec((None, d, tf), lambda i, j: (layer, 0, nf + j)),
            pl.BlockSpec((None, tf, d), lambda i, j: (layer, j, 0)),
            pl.BlockSpec((1, d), lambda i, j: (0, 0)),
        ],
        out_specs=pl.BlockSpec((tm, d), lambda i, j: (i, 0)),
        scratch_shapes=[pltpu.VMEM((tm, d), BF16)],
        compiler_params=_params("parallel", "arbitrary", vmem_limit_bytes=VMEM_LIMIT_BYTES_LARGE),
        name="ffn",
    )(h, g.reshape(1, d), w_gu, w_gu, w_d, gf.reshape(1, d))


def _mem_out_proj_kernel(mix_ref, qm_ref, mk_ref, mv_ref, w1_ref, w2_ref, h_ref, gu_ref, wd_ref,
                         o_ref, gu_b_ref, wd_b_ref, *, scale):
    gu_b_ref[...] = gu_ref[...].astype(BF16)
    wd_b_ref[...] = wd_ref[...].astype(BF16)
    acc = h_ref[...] + jnp.dot(mix_ref[...], w1_ref[...], preferred_element_type=F32)
    heads = []
    for hd in range(MEM_HEADS):
        cols = slice(hd * HEAD_DIM, (hd + 1) * HEAD_DIM)
        s = lax.dot_general(qm_ref[:, cols], mk_ref[:, cols], (((1,), (1,)), ((), ())),
                            preferred_element_type=F32)
        m = jnp.max(s, axis=1, keepdims=True)
        p = jnp.exp2((s - m) * (scale * LOG2_E))
        l = jnp.sum(p, axis=1, keepdims=True)
        o = jnp.dot(p.astype(BF16), mv_ref[:, cols], preferred_element_type=F32)
        heads.append((o / l).astype(BF16))
    mem_out = jnp.concatenate(heads, axis=1)
    o_ref[...] = acc + jnp.dot(mem_out, w2_ref[...], preferred_element_type=F32)


def _cast_rows(total_rows, steps):
    for rows in range(16, total_rows + 1, 16):
        if total_rows % rows == 0 and total_rows // rows <= steps:
            return rows
    raise ValueError((total_rows, steps))


def _mem_out_proj(mix, q_src, q_block, mkv, kv_block, w_out, layer, h, ffn_w_gu, ffn_w_d, ffn_layer, *, tm):
    m, d = h.shape
    bsz, ml, _ = mkv.shape
    k1, k2 = mix.shape[1], MEM_HEADS * HEAD_DIM
    assert k1 % k2 == 0 and w_out.shape[1] == k1 + k2
    tm = _tile(m // bsz, tm)
    steps_per_batch = (m // bsz) // tm
    steps = m // tm
    gu_rows, wd_rows = _cast_rows(ffn_w_gu.shape[1], steps), _cast_rows(ffn_w_d.shape[1], steps)
    gu_last, wd_last = ffn_w_gu.shape[1] // gu_rows - 1, ffn_w_d.shape[1] // wd_rows - 1
    return pl.pallas_call(
        functools.partial(_mem_out_proj_kernel, scale=HEAD_DIM ** -0.5),
        out_shape=(jax.ShapeDtypeStruct((m, d), F32),
                   jax.ShapeDtypeStruct(ffn_w_gu.shape[1:], BF16),
                   jax.ShapeDtypeStruct(ffn_w_d.shape[1:], BF16)),
        grid=(steps,),
        in_specs=[
            pl.BlockSpec((tm, k1), lambda i: (i, 0)),
            pl.BlockSpec((tm, k2), lambda i: (i, q_block)),
            pl.BlockSpec((None, ml, k2), lambda i: (i // steps_per_batch, 0, kv_block)),
            pl.BlockSpec((None, ml, k2), lambda i: (i // steps_per_batch, 0, kv_block + 1)),
            pl.BlockSpec((None, k1, d), lambda i: (layer, 0, 0)),
            pl.BlockSpec((None, k2, d), lambda i: (layer, k1 // k2, 0)),
            pl.BlockSpec((tm, d), lambda i: (i, 0)),
            pl.BlockSpec((None, gu_rows, ffn_w_gu.shape[2]), lambda i: (ffn_layer, jnp.minimum(i, gu_last), 0)),
            pl.BlockSpec((None, wd_rows, ffn_w_d.shape[2]), lambda i: (ffn_layer, jnp.minimum(i, wd_last), 0)),
        ],
        out_specs=(pl.BlockSpec((tm, d), lambda i: (i, 0)),
                   pl.BlockSpec((gu_rows, ffn_w_gu.shape[2]), lambda i: (jnp.minimum(i, gu_last), 0)),
                   pl.BlockSpec((wd_rows, ffn_w_d.shape[2]), lambda i: (jnp.minimum(i, wd_last), 0))),
        compiler_params=_params("arbitrary"),
        name="mem_out_proj",
    )(mix, q_src, mkv, mkv, w_out, w_out, h, ffn_w_gu, ffn_w_d)


def _sb_attn_kernel(q_ref, k_ref, v_ref, tri_ref, o_ref, acc_ref, carry_ref, *, blk):
    for hd in range(q_ref.shape[1] // HEAD_DIM):
        cols = slice(hd * HEAD_DIM, (hd + 1) * HEAD_DIM)
        _sb_attn_head(q_ref.at[:, cols], k_ref.at[:, cols], v_ref.at[:, cols], tri_ref, o_ref.at[:, cols],
                      acc_ref, carry_ref, blk=blk)


def _sb_attn_head(q_ref, k_ref, v_ref, tri_ref, o_ref, acc_ref, carry_ref, *, blk):
    tri2 = tri_ref[...]
    row = lax.broadcasted_iota(jnp.int32, (blk, blk), 0)
    col = lax.broadcasted_iota(jnp.int32, (blk, blk), 1)
    strictly_past = col < row
    nq = q_ref.shape[0] // blk

    def rows(i):
        return slice(i * blk, (i + 1) * blk)

    def log_keep(nz):
        neg_abs = lax.bitcast_convert_type(
            lax.bitcast_convert_type(nz, jnp.uint32) | jnp.uint32(0x80000000), F32)
        return jnp.minimum(nz, 0.0) - jnp.log2(1.0 + jnp.exp2(neg_abs))

    def block_weights(nz, lk, carry, diagonal):
        if diagonal:
            lk = jnp.where(strictly_past, lk, 0.0)
        hi = lk.astype(BF16)
        lo = (lk - hi.astype(F32)).astype(BF16)
        log_w = jnp.dot(jnp.concatenate([hi, lo], axis=1), tri2, preferred_element_type=F32) - nz
        if carry is not None:
            log_w = log_w + carry
        w = jnp.exp2(log_w)
        if diagonal:
            w = jnp.where(strictly_past, w, 0.0)
        block_sum = jnp.sum(lk, axis=1, keepdims=True)
        return w.astype(BF16), (block_sum if carry is None else carry + block_sum)

    def first_key_block(i):
        return max(i - SB_NEAR_KEY_BLOCKS + 1, 0)

    def logits(i):
        keys = slice(first_key_block(i) * blk, (i + 1) * blk)
        return lax.dot_general(q_ref[rows(i), :], k_ref[keys, :], (((1,), (1,)), ((), ())),
                               preferred_element_type=F32)

    def finish(i, nz, lkeep):
        j0 = first_key_block(i)
        w_blocks, carry = [None] * (i + 1 - j0), None
        for j in range(i, j0 - 1, -1):
            cols = slice((j - j0) * blk, (j - j0 + 1) * blk)
            w_blocks[j - j0], carry = block_weights(nz[:, cols], lkeep[:, cols], carry, j == i)
        w = w_blocks[0] if len(w_blocks) == 1 else jnp.concatenate(w_blocks, axis=1)
        acc = jnp.dot(w, v_ref[j0 * blk:(i + 1) * blk, :], preferred_element_type=F32)
        o_ref[rows(i), :] = acc.astype(o_ref.dtype)
        if j0 > 0:
            acc_ref[rows(i), :] = acc
            carry_ref[rows(i), :] = jnp.broadcast_to(carry, (blk, LANES))
            q32 = q_ref[rows(i), :].astype(F32)
            logit_bound[i] = jnp.sqrt(jnp.max(jnp.sum(q32 * q32, axis=1, keepdims=True)) * k_sq) * 1.01 + 1.0
            carry_max[i] = jnp.max(carry)

    if nq > SB_NEAR_KEY_BLOCKS:
        k32 = k_ref[...].astype(F32)
        k_sq = jnp.max(jnp.sum(k32 * k32, axis=1, keepdims=True))
    logit_bound, carry_max = {}, {}

    order = list(range(nq - 1, -1, -1))
    nz_of, lk_of = {}, {}
    for step in range(nq + 2):
        if step < nq:
            nz_of[order[step]] = logits(order[step])
        if 0 <= step - 1 < nq:
            lk_of[order[step - 1]] = log_keep(nz_of[order[step - 1]])
        if 0 <= step - 2 < nq:
            i = order[step - 2]
            finish(i, nz_of.pop(i), lk_of.pop(i))

    far_blocks = range(SB_NEAR_KEY_BLOCKS, nq)

    def far_part():
        for i in far_blocks:
            def more(state, i=i):
                j, c_max = state
                return jnp.logical_and(j >= 0, c_max + logit_bound[i] >= SB_ZERO_WEIGHT_LOG2)

            def take_block(state, i=i):
                j, _ = state
                keys = pl.ds(pl.multiple_of(j * blk, blk), blk)
                nz = lax.dot_general(q_ref[rows(i), :], k_ref[keys, :], (((1,), (1,)), ((), ())),
                                     preferred_element_type=F32)
                w, carry = block_weights(nz, log_keep(nz), carry_ref[rows(i), 0:1], False)
                acc_ref[rows(i), :] += jnp.dot(w, v_ref[keys, :], preferred_element_type=F32)
                carry_ref[rows(i), :] = jnp.broadcast_to(carry, (blk, LANES))
                return j - 1, jnp.max(carry)

            j_end, _ = lax.while_loop(more, take_block, (jnp.int32(i - SB_NEAR_KEY_BLOCKS), carry_max[i]))

            @pl.when(j_end < i - SB_NEAR_KEY_BLOCKS)
            def _(i=i):
                o_ref[rows(i), :] = acc_ref[rows(i), :].astype(o_ref.dtype)

    if far_blocks:
        needed = functools.reduce(jnp.maximum, [carry_max[i] + logit_bound[i] for i in far_blocks])
        pl.when(needed >= SB_ZERO_WEIGHT_LOG2)(far_part)


def _sb_attention(proj, n_heads, *, blk, heads_per_step):
    b, s, _ = proj.shape
    blk = _tile(s, blk)
    tri = (jnp.arange(blk)[:, None] >= jnp.arange(blk)[None, :]).astype(BF16)
    tri2 = jnp.concatenate([tri, tri], axis=0)
    groups, width = n_heads // _tile(n_heads, heads_per_step), _tile(n_heads, heads_per_step) * HEAD_DIM
    return pl.pallas_call(
        functools.partial(_sb_attn_kernel, blk=blk),
        out_shape=jax.ShapeDtypeStruct((b, s, n_heads * HEAD_DIM), BF16),
        grid=(b, groups),
        in_specs=[
            pl.BlockSpec((None, s, width), lambda bi, h: (bi, 0, h)),
            pl.BlockSpec((None, s, width), lambda bi, h: (bi, 0, groups + h)),
            pl.BlockSpec((None, s, width), lambda bi, h: (bi, 0, 2 * groups + h)),
            pl.BlockSpec((2 * blk, blk), lambda bi, h: (0, 0)),
        ],
        out_specs=pl.BlockSpec((None, s, width), lambda bi, h: (bi, 0, h)),
        scratch_shapes=[pltpu.VMEM((s, HEAD_DIM), F32), pltpu.VMEM((s, LANES), F32)],
        compiler_params=_params("parallel", "parallel"),
        name="sb_attention",
    )(proj, proj, proj, tri2)


def _mla_attn_kernel(q_ref, k_ref, v_ref, o_ref, *, blk, scale):
    for hd in range(v_ref.shape[1] // MLA_V_DIM):
        qk_cols = slice(hd * MLA_QK_PAD, (hd + 1) * MLA_QK_PAD)
        v_cols = slice(hd * MLA_V_DIM, (hd + 1) * MLA_V_DIM)
        _mla_attn_head(q_ref.at[:, qk_cols], k_ref.at[:, qk_cols], v_ref.at[:, v_cols], o_ref.at[:, v_cols],
                       blk=blk, scale=scale)


def _mla_attn_head(q_ref, k_ref, v_ref, o_ref, *, blk, scale):
    row = lax.broadcasted_iota(jnp.int32, (blk, blk), 0)
    col = lax.broadcasted_iota(jnp.int32, (blk, blk), 1)
    causal = col <= row
    nq = q_ref.shape[0] // blk

    def scores(i):
        q = q_ref[i * blk:(i + 1) * blk, :]
        return lax.dot_general(q, k_ref[0:(i + 1) * blk, :], (((1,), (1,)), ((), ())),
                               preferred_element_type=F32)

    def probs(i, s):
        diag = jnp.where(causal, s[:, i * blk:], -jnp.inf)
        s = diag if i == 0 else jnp.concatenate([s[:, :i * blk], diag], axis=1)
        m = jnp.max(s, axis=1, keepdims=True)
        p = jnp.exp2((s - m) * (scale * LOG2_E))
        return p.astype(BF16), jnp.sum(p, axis=1, keepdims=True)

    def finish(i, p, l):
        o = jnp.dot(p, v_ref[0:(i + 1) * blk, :], preferred_element_type=F32)
        o_ref[i * blk:(i + 1) * blk, :] = (o / l).astype(o_ref.dtype)

    order = list(range(nq - 1, -1, -1))
    s_of, p_of = {}, {}
    for step in range(nq + 2):
        if step < nq:
            s_of[order[step]] = scores(order[step])
        if 0 <= step - 1 < nq:
            i = order[step - 1]
            p_of[i] = probs(i, s_of.pop(i))
        if 0 <= step - 2 < nq:
            i = order[step - 2]
            finish(i, *p_of.pop(i))


def _mla_attention(q_cat, k_cat, v, n_heads, *, blk, heads_per_step):
    b, s, _ = q_cat.shape
    blk = _tile(s, blk)
    scale = (MLA_NOPE_DIM + MLA_ROPE_DIM) ** -0.5
    per_step = _tile(n_heads, heads_per_step)
    return pl.pallas_call(
        functools.partial(_mla_attn_kernel, blk=blk, scale=scale),
        out_shape=jax.ShapeDtypeStruct((b, s, n_heads * MLA_V_DIM), BF16),
        grid=(b, n_heads // per_step),
        in_specs=[
            pl.BlockSpec((None, s, per_step * MLA_QK_PAD), lambda bi, h: (bi, 0, h)),
            pl.BlockSpec((None, s, per_step * MLA_QK_PAD), lambda bi, h: (bi, 0, h)),
            pl.BlockSpec((None, s, per_step * MLA_V_DIM), lambda bi, h: (bi, 0, h)),
        ],
        out_specs=pl.BlockSpec((None, s, per_step * MLA_V_DIM), lambda bi, h: (bi, 0, h)),
        compiler_params=_params("parallel", "parallel"),
        name="mla_attention",
    )(q_cat, k_cat, v)


def _latent_kv_kernel(h_ref, g_ref, wdkv_ref, gl_ref, wukv_ref, ca_ref, cb_ref,
                      kcat_ref, v_ref, *, n_heads, rank):
    rows = h_ref.shape[0] // ROW_SPLITS
    for c in range(ROW_SPLITS):
        r = slice(c * rows, (c + 1) * rows)
        xn = _rms(h_ref[r, :], g_ref[...]).astype(BF16)
        ckv = jnp.dot(xn, wdkv_ref[...], preferred_element_type=F32)
        c_latent = _rms(ckv[:, :rank], gl_ref[...]).astype(BF16)
        k_rope = _rope(ckv[:, rank:], ca_ref[r, :], cb_ref[r, :]).astype(BF16)
        for h in range(n_heads):
            kv = jnp.dot(c_latent, wukv_ref[:, h * 256:(h + 1) * 256], preferred_element_type=F32)
            kcat_ref[r, h * MLA_QK_PAD:h * MLA_QK_PAD + MLA_NOPE_DIM] = kv[:, :MLA_NOPE_DIM].astype(BF16)
            kcat_ref[r, h * MLA_QK_PAD + MLA_NOPE_DIM:(h + 1) * MLA_QK_PAD] = k_rope
            v_ref[r, h * MLA_V_DIM:(h + 1) * MLA_V_DIM] = kv[:, MLA_NOPE_DIM:].astype(BF16)


def _latent_kv(h, g, w_dkv_pad, g_latent, w_ukv, ca, cb, n_heads, *, tm):
    m, d = h.shape
    rank = g_latent.shape[0]
    tm = _tile(m, tm)
    row = lambda i: (i, 0)
    fixed = lambda i: (0, 0)
    return pl.pallas_call(
        functools.partial(_latent_kv_kernel, n_heads=n_heads, rank=rank),
        out_shape=(jax.ShapeDtypeStruct((m, n_heads * MLA_QK_PAD), BF16),
                   jax.ShapeDtypeStruct((m, n_heads * MLA_V_DIM), BF16)),
        grid=(m // tm,),
        in_specs=[
            pl.BlockSpec((tm, d), row),
            pl.BlockSpec((1, d), fixed),
            pl.BlockSpec(w_dkv_pad.shape, fixed),
            pl.BlockSpec((1, rank), fixed),
            pl.BlockSpec(w_ukv.shape, fixed),
            pl.BlockSpec((tm, LANES), row),
            pl.BlockSpec((tm, LANES), row),
        ],
        out_specs=(pl.BlockSpec((tm, n_heads * MLA_QK_PAD), row),
                   pl.BlockSpec((tm, n_heads * MLA_V_DIM), row)),
        compiler_params=_params("parallel"),
        name="latent_kv",
    )(h, g.reshape(1, d), w_dkv_pad, g_latent.reshape(1, rank), w_ukv, ca, cb)


def _mla_in_kernel(h_ref, g_ref, win_ref, gq_ref, wuq_ref, ca_ref, cb_ref,
                   qcat_ref, memq_ref, *, n_heads, rank):
    rows = h_ref.shape[0] // ROW_SPLITS
    for c in range(ROW_SPLITS):
        r = slice(c * rows, (c + 1) * rows)
        xn = _rms(h_ref[r, :], g_ref[...]).astype(BF16)
        proj = jnp.dot(xn, win_ref[...], preferred_element_type=F32)
        memq_ref[r, :] = proj[:, rank:].astype(BF16)
        c_q = _rms(proj[:, :rank], gq_ref[...]).astype(BF16)
        ca, cb = ca_ref[r, :], cb_ref[r, :]
        for h in range(n_heads):
            q = jnp.dot(c_q, wuq_ref[:, h * MLA_QK_PAD:(h + 1) * MLA_QK_PAD], preferred_element_type=F32)
            qcat_ref[r, h * MLA_QK_PAD:h * MLA_QK_PAD + MLA_NOPE_DIM] = q[:, :MLA_NOPE_DIM].astype(BF16)
            qcat_ref[r, h * MLA_QK_PAD + MLA_NOPE_DIM:(h + 1) * MLA_QK_PAD] = (
                _rope(q[:, MLA_NOPE_DIM:], ca, cb).astype(BF16))


def _mla_in(h, g, w_in, g_q, w_uq_pad, layer, ca, cb, n_heads, *, tm):
    m, d = h.shape
    rank = g_q.shape[0]
    n_memq = w_in.shape[2] - rank
    tm = _tile(m, tm)
    row = lambda i: (i, 0)
    fixed = lambda i: (0, 0)
    stacked = lambda i: (layer, 0, 0)
    return pl.pallas_call(
        functools.partial(_mla_in_kernel, n_heads=n_heads, rank=rank),
        out_shape=(jax.ShapeDtypeStruct((m, n_heads * MLA_QK_PAD), BF16),
                   jax.ShapeDtypeStruct((m, n_memq), BF16)),
        grid=(m // tm,),
        in_specs=[
            pl.BlockSpec((tm, d), row),
            pl.BlockSpec((1, d), fixed),
            pl.BlockSpec((None,) + w_in.shape[1:], stacked),
            pl.BlockSpec((1, rank), fixed),
            pl.BlockSpec((None,) + w_uq_pad.shape[1:], stacked),
            pl.BlockSpec((tm, LANES), row),
            pl.BlockSpec((tm, LANES), row),
        ],
        out_specs=(pl.BlockSpec((tm, n_heads * MLA_QK_PAD), row),
                   pl.BlockSpec((tm, n_memq), row)),
        compiler_params=_params("parallel"),
        name="mla_in",
    )(h, g.reshape(1, d), w_in, g_q.reshape(1, rank), w_uq_pad, ca, cb)


def _rope_tables(positions):
    half = MLA_ROPE_DIM // 2
    inv_freq = ROPE_THETA ** (-jnp.arange(half, dtype=F32) / half)
    ang = positions.reshape(-1).astype(F32)[:, None] * inv_freq
    cos, sin = jnp.cos(ang), jnp.sin(ang)
    zeros = jnp.zeros((ang.shape[0], LANES - MLA_ROPE_DIM), F32)
    ca = jnp.concatenate([cos, cos, zeros], axis=-1)
    cb = jnp.concatenate([-sin, sin, zeros], axis=-1)
    return ca, cb


def kernel(x, mem, positions, attn_norm_g, ffn_norm_g, a_w_in, a_w_out, b_w_in, b_q_norm_g, b_w_uq, b_w_out, mem_norm_g, w_mem_kv, kv_norm_g, w_dkv, kv_latent_g, w_ukv, ffn_w_gu, ffn_w_down, final_norm_g):
    b, s, d = x.shape
    mem_len = mem.shape[1]
    depth = attn_norm_g.shape[0]
    n_a = a_w_in.shape[0]
    n_b = b_w_in.shape[0]
    mq_w = MEM_HEADS * HEAD_DIM
    sb_heads = (a_w_in.shape[2] - mq_w) // (3 * HEAD_DIM)
    q_rank = b_q_norm_g.shape[1]
    kv_rank = kv_latent_g.shape[0]
    mla_heads = b_w_uq.shape[2] // (MLA_NOPE_DIM + MLA_ROPE_DIM)

    a_w_in_b = a_w_in.astype(BF16)
    a_w_out_b = a_w_out.astype(BF16)
    b_w_in_b = b_w_in.astype(BF16)
    b_w_out_b = b_w_out.astype(BF16)
    w_ukv_b = w_ukv.astype(BF16)
    w_mem_all = jnp.transpose(w_mem_kv, (1, 0, 2)).reshape(d, depth * 2 * mq_w).astype(BF16)
    w_dkv_pad = jnp.pad(w_dkv, ((0, 0), (0, LANES - MLA_ROPE_DIM))).astype(BF16)
    w_uq_pad = jnp.pad(
        b_w_uq.reshape(n_b, q_rank, mla_heads, MLA_NOPE_DIM + MLA_ROPE_DIM),
        ((0, 0), (0, 0), (0, 0), (0, MLA_QK_PAD - MLA_NOPE_DIM - MLA_ROPE_DIM)),
    ).reshape(n_b, q_rank, mla_heads * MLA_QK_PAD).astype(BF16)
    ca, cb = _rope_tables(positions)
    a_col_scale = jnp.concatenate([jnp.full((sb_heads * HEAD_DIM,), SB_Q_PRESCALE, F32),
                                   jnp.ones((a_w_in.shape[2] - sb_heads * HEAD_DIM,), F32)])

    h = x.reshape(b * s, d)
    mkv = _norm_matmul(mem.reshape(b * mem_len, d), mem_norm_g, w_mem_all[None], 0,
                       jnp.ones((w_mem_all.shape[1],), F32), tm=NORM_MATMUL_ROWS, tn=NORM_MATMUL_COLS)
    mkv = mkv.reshape(b, mem_len, depth * 2 * mq_w)

    k_cat = v_lat = None
    for layer in range(depth):
        if layer == n_a:
            k_cat, v_lat = _latent_kv(h, kv_norm_g, w_dkv_pad, kv_latent_g, w_ukv_b, ca, cb,
                                      mla_heads, tm=LATENT_KV_ROWS)
            k_cat = k_cat.reshape(b, s, -1)
            v_lat = v_lat.reshape(b, s, -1)
        if layer < n_a:
            proj = _norm_matmul(h, attn_norm_g[layer], a_w_in_b, layer, a_col_scale,
                                tm=NORM_MATMUL_ROWS, tn=NORM_MATMUL_COLS)
            mix = _sb_attention(proj.reshape(b, s, -1), sb_heads, blk=ATTENTION_BLOCK,
                                heads_per_step=SB_HEADS_PER_STEP)
            q_src, q_block = proj, (3 * sb_heads * HEAD_DIM) // mq_w
            w_out, li = a_w_out_b, layer
        else:
            li = layer - n_a
            q_cat, mem_q = _mla_in(h, attn_norm_g[layer], b_w_in_b, b_q_norm_g[li], w_uq_pad, li,
                                   ca, cb, mla_heads, tm=MLA_IN_ROWS)
            mix = _mla_attention(q_cat.reshape(b, s, -1), k_cat, v_lat, mla_heads, blk=ATTENTION_BLOCK,
                                 heads_per_step=MLA_HEADS_PER_STEP)
            q_src, q_block = mem_q, 0
            w_out = b_w_out_b
        h, w_gu_b, w_d_b = _mem_out_proj(mix.reshape(b * s, -1), q_src, q_block, mkv, 2 * layer, w_out, li, h,
                                         ffn_w_gu, ffn_w_down, layer, tm=MEM_OUT_PROJ_ROWS)
        h = _ffn(h, ffn_norm_g[layer], w_gu_b[None], w_d_b[None], 0, final_norm_g,
                 final_norm=(layer == depth - 1), tm=FFN_ROWS, tf=FFN_HIDDEN)
    return h.reshape(b, s, d)
```

```python
import functools

import jax
import jax.numpy as jnp
from jax import lax
from jax.experimental import pallas as pl
from jax.experimental.pallas import tpu as pltpu

HEAD_DIM = 128
MEM_HEADS = 4
MLA_NOPE_DIM = 128
MLA_ROPE_DIM = 64
MLA_V_DIM = 128
ROPE_THETA = 10000.0
RMS_EPS = 1e-6
LOG2_E = 1.4426950408889634
SB_Q_PRESCALE = -(HEAD_DIM ** -0.5) * LOG2_E
SB_NEAR_KEY_BLOCKS = 2
SB_ZERO_WEIGHT_LOG2 = -140.0

LANES = 128
MLA_QK_PAD = 2 * LANES
NORM_CHUNK_ROWS = 256
FFN_FIRST_STEP_SPLITS = 2
NORM_MATMUL_FIRST_STEP_SPLITS = 2
ROW_SPLITS = 2

VMEM_LIMIT_BYTES = 48 * 1024 * 1024
VMEM_LIMIT_BYTES_LARGE = 58 * 1024 * 1024
NORM_MATMUL_ROWS, NORM_MATMUL_COLS = 512, 1024
FFN_ROWS, FFN_HIDDEN = 1024, 512
MEM_OUT_PROJ_ROWS = 512
MLA_IN_ROWS = 1024
LATENT_KV_ROWS = 1024
ATTENTION_BLOCK = 256
SB_HEADS_PER_STEP = 2
MLA_HEADS_PER_STEP = 4

BF16 = jnp.bfloat16
F32 = jnp.float32


def _params(*semantics, vmem_limit_bytes=VMEM_LIMIT_BYTES):
    return pltpu.CompilerParams(dimension_semantics=semantics, vmem_limit_bytes=vmem_limit_bytes)


def _tile(n, pref):
    t = min(n, pref)
    assert n % t == 0, (n, pref)
    return t


def _rms(x, g):
    return (x * lax.rsqrt(jnp.mean(x * x, axis=-1, keepdims=True) + RMS_EPS)) * g


def _rmsnorm_rows(x_ref, g_ref, out_ref):
    rows = x_ref.shape[0]
    chunk = min(rows, NORM_CHUNK_ROWS)
    g = g_ref[...]

    def body(c, carry):
        r = pl.multiple_of(c * chunk, chunk)
        out_ref[pl.ds(r, chunk), :] = _rms(x_ref[pl.ds(r, chunk), :], g).astype(out_ref.dtype)
        return carry

    lax.fori_loop(0, rows // chunk, body, 0)


def _rope(x, ca, cb):
    partner = pltpu.roll(x, 96, 1) + pltpu.roll(x, 32, 1)
    return x * ca + partner * cb


def _norm_matmul_kernel(h_ref, g_ref, w_ref, cs_ref, o_ref, xn_ref, *, tn):
    def column_tile(xn, j):
        cols = slice(j * tn, (j + 1) * tn)
        return (jnp.dot(xn, w_ref[:, cols], preferred_element_type=F32) * cs_ref[:, cols]).astype(o_ref.dtype)

    rows = h_ref.shape[0] // NORM_MATMUL_FIRST_STEP_SPLITS
    g = g_ref[...]
    for c in range(NORM_MATMUL_FIRST_STEP_SPLITS):
        r = slice(c * rows, (c + 1) * rows)
        xn = _rms(h_ref[r, :], g).astype(BF16)
        xn_ref[r, :] = xn
        o_ref[r, 0:tn] = column_tile(xn, 0)
    for j in range(1, o_ref.shape[1] // tn):
        o_ref[:, j * tn:(j + 1) * tn] = column_tile(xn_ref[...], j)


def _norm_matmul(h, g, w, layer, col_scale, *, tm, tn):
    m, d = h.shape
    n = w.shape[2]
    tm, tn = _tile(m, tm), _tile(n, tn)
    return pl.pallas_call(
        functools.partial(_norm_matmul_kernel, tn=tn),
        out_shape=jax.ShapeDtypeStruct((m, n), BF16),
        grid=(m // tm,),
        in_specs=[
            pl.BlockSpec((tm, d), lambda i: (i, 0)),
            pl.BlockSpec((1, d), lambda i: (0, 0)),
            pl.BlockSpec((None, d, n), lambda i: (layer, 0, 0), pipeline_mode=pl.Buffered(1)),
            pl.BlockSpec((1, n), lambda i: (0, 0)),
        ],
        out_specs=pl.BlockSpec((tm, n), lambda i: (i, 0)),
        scratch_shapes=[pltpu.VMEM((tm, d), BF16)],
        compiler_params=_params("parallel", vmem_limit_bytes=VMEM_LIMIT_BYTES_LARGE),
        name="norm_matmul",
    )(h, g.reshape(1, d), w, col_scale.reshape(1, n))


def _ffn_kernel(h_ref, g_ref, wg_ref, wu_ref, wd_ref, gf_ref, o_ref, xn_ref, *, final_norm):
    j = pl.program_id(1)

    def hidden_tile(xn):
        gate = jnp.dot(xn, wg_ref[...], preferred_element_type=F32)
        up = jnp.dot(xn, wu_ref[...], preferred_element_type=F32)
        act = (gate * jax.nn.sigmoid(gate) * up).astype(BF16)
        return jnp.dot(act, wd_ref[...], preferred_element_type=F32)

    @pl.when(j == 0)
    def _():
        half = h_ref.shape[0] // FFN_FIRST_STEP_SPLITS
        g = g_ref[...]
        for c in range(FFN_FIRST_STEP_SPLITS):
            h = h_ref[c * half:(c + 1) * half, :]
            xn = _rms(h, g).astype(BF16)
            xn_ref[c * half:(c + 1) * half, :] = xn
            o_ref[c * half:(c + 1) * half, :] = h + hidden_tile(xn)

    @pl.when(j > 0)
    def _():
        o_ref[...] += hidden_tile(xn_ref[...])

    if final_norm:
        @pl.when(j == pl.num_programs(1) - 1)
        def _():
            _rmsnorm_rows(o_ref, gf_ref, o_ref)


def _ffn(h, g, w_gu, w_d, layer, gf, *, final_norm, tm, tf):
    m, d = h.shape
    f = w_d.shape[1]
    tm, tf = _tile(m, tm), _tile(f, tf)
    nf = f // tf
    return pl.pallas_call(
        functools.partial(_ffn_kernel, final_norm=final_norm),
        out_shape=jax.ShapeDtypeStruct((m, d), F32),
        grid=(m // tm, nf),
        in_specs=[
            pl.BlockSpec((tm, d), lambda i, j: (i, 0)),
            pl.BlockSpec((1, d), lambda i, j: (0, 0)),
            pl.BlockSpec((None, d, tf), lambda i, j: (layer, 0, j)),
            pl.BlockSpec((None, d, tf), lambda i, j: (layer, 0, nf + j)),
            pl.BlockSpec((None, tf, d), lambda i, j: (layer, j, 0)),
            pl.BlockSpec((1, d), lambda i, j: (0, 0)),
        ],
        out_specs=pl.BlockSpec((tm, d), lambda i, j: (i, 0)),
        scratch_shapes=[pltpu.VMEM((tm, d), BF16)],
        compiler_params=_params("parallel", "arbitrary", vmem_limit_bytes=VMEM_LIMIT_BYTES_LARGE),
        name="ffn",
    )(h, g.reshape(1, d), w_gu, w_gu, w_d, gf.reshape(1, d))


def _mem_out_proj_kernel(mix_ref, qm_ref, mk_ref, mv_ref, w1_ref, w2_ref, h_ref, gu_ref, wd_ref,
                         o_ref, gu_b_ref, wd_b_ref, *, scale):
    gu_b_ref[...] = gu_ref[...].astype(BF16)
    wd_b_ref[...] = wd_ref[...].astype(BF16)
    acc = h_ref[...] + jnp.dot(mix_ref[...], w1_ref[...], preferred_element_type=F32)
    heads = []
    for hd in range(MEM_HEADS):
        cols = slice(hd * HEAD_DIM, (hd + 1) * HEAD_DIM)
        s = lax.dot_general(qm_ref[:, cols], mk_ref[:, cols], (((1,), (1,)), ((), ())),
                            preferred_element_type=F32)
        m = jnp.max(s, axis=1, keepdims=True)
        p = jnp.exp2((s - m) * (scale * LOG2_E))
        l = jnp.sum(p, axis=1, keepdims=True)
        o = jnp.dot(p.astype(BF16), mv_ref[:, cols], preferred_element_type=F32)
        heads.append((o / l).astype(BF16))
    mem_out = jnp.concatenate(heads, axis=1)
    o_ref[...] = acc + jnp.dot(mem_out, w2_ref[...], preferred_element_type=F32)


def _cast_rows(total_rows, steps):
    for rows in range(16, total_rows + 1, 16):
        if total_rows % rows == 0 and total_rows // rows <= steps:
            return rows
    raise ValueError((total_rows, steps))


def _mem_out_proj(mix, q_src, q_block, mkv, kv_block, w_out, layer, h, ffn_w_gu, ffn_w_d, ffn_layer, *, tm):
    m, d = h.shape
    bsz, ml, _ = mkv.shape
    k1, k2 = mix.shape[1], MEM_HEADS * HEAD_DIM
    assert k1 % k2 == 0 and w_out.shape[1] == k1 + k2
    tm = _tile(m // bsz, tm)
    steps_per_batch = (m // bsz) // tm
    steps = m // tm
    gu_rows, wd_rows = _cast_rows(ffn_w_gu.shape[1], steps), _cast_rows(ffn_w_d.shape[1], steps)
    gu_last, wd_last = ffn_w_gu.shape[1] // gu_rows - 1, ffn_w_d.shape[1] // wd_rows - 1
    return pl.pallas_call(
        functools.partial(_mem_out_proj_kernel, scale=HEAD_DIM ** -0.5),
        out_shape=(jax.ShapeDtypeStruct((m, d), F32),
                   jax.ShapeDtypeStruct(ffn_w_gu.shape[1:], BF16),
                   jax.ShapeDtypeStruct(ffn_w_d.shape[1:], BF16)),
        grid=(steps,),
        in_specs=[
            pl.BlockSpec((tm, k1), lambda i: (i, 0)),
            pl.BlockSpec((tm, k2), lambda i: (i, q_block)),
            pl.BlockSpec((None, ml, k2), lambda i: (i // steps_per_batch, 0, kv_block)),
            pl.BlockSpec((None, ml, k2), lambda i: (i // steps_per_batch, 0, kv_block + 1)),
            pl.BlockSpec((None, k1, d), lambda i: (layer, 0, 0)),
            pl.BlockSpec((None, k2, d), lambda i: (layer, k1 // k2, 0)),
            pl.BlockSpec((tm, d), lambda i: (i, 0)),
            pl.BlockSpec((None, gu_rows, ffn_w_gu.shape[2]), lambda i: (ffn_layer, jnp.minimum(i, gu_last), 0)),
            pl.BlockSpec((None, wd_rows, ffn_w_d.shape[2]), lambda i: (ffn_layer, jnp.minimum(i, wd_last), 0)),
        ],
        out_specs=(pl.BlockSpec((tm, d), lambda i: (i, 0)),
                   pl.BlockSpec((gu_rows, ffn_w_gu.shape[2]), lambda i: (jnp.minimum(i, gu_last), 0)),
                   pl.BlockSpec((wd_rows, ffn_w_d.shape[2]), lambda i: (jnp.minimum(i, wd_last), 0))),
        compiler_params=_params("arbitrary"),
        name="mem_out_proj",
    )(mix, q_src, mkv, mkv, w_out, w_out, h, ffn_w_gu, ffn_w_d)


def _sb_attn_kernel(q_ref, k_ref, v_ref, tri_ref, o_ref, acc_ref, carry_ref, *, blk):
    tri2 = tri_ref[...]
    row = lax.broadcasted_iota(jnp.int32, (blk, blk), 0)
    col = lax.broadcasted_iota(jnp.int32, (blk, blk), 1)
    strictly_past = col < row
    nq = q_ref.shape[0] // blk
    heads = q_ref.shape[1] // HEAD_DIM

    def rows(i):
        return slice(i * blk, (i + 1) * blk)

    def lanes(hd):
        return slice(hd * HEAD_DIM, (hd + 1) * HEAD_DIM)

    def log_keep(nz):
        neg_abs = lax.bitcast_convert_type(
            lax.bitcast_convert_type(nz, jnp.uint32) | jnp.uint32(0x80000000), F32)
        return jnp.minimum(nz, 0.0) - jnp.log2(1.0 + jnp.exp2(neg_abs))

    def block_weights(nz, lk, carry, diagonal):
        if diagonal:
            lk = jnp.where(strictly_past, lk, 0.0)
        hi = lk.astype(BF16)
        lo = (lk - hi.astype(F32)).astype(BF16)
        log_w = jnp.dot(jnp.concatenate([hi, lo], axis=1), tri2, preferred_element_type=F32) - nz
        if carry is not None:
            log_w = log_w + carry
        w = jnp.exp2(log_w)
        if diagonal:
            w = jnp.where(strictly_past, w, 0.0)
        block_sum = jnp.sum(lk, axis=1, keepdims=True)
        return w.astype(BF16), (block_sum if carry is None else carry + block_sum)

    def first_key_block(i):
        return max(i - SB_NEAR_KEY_BLOCKS + 1, 0)

    def logits(i, hd):
        keys = slice(first_key_block(i) * blk, (i + 1) * blk)
        return lax.dot_general(q_ref[rows(i), lanes(hd)], k_ref[keys, lanes(hd)], (((1,), (1,)), ((), ())),
                               preferred_element_type=F32)

    def finish(i, hd, nz, lkeep):
        j0 = first_key_block(i)
        w_blocks, carry = [None] * (i + 1 - j0), None
        for j in range(i, j0 - 1, -1):
            cols = slice((j - j0) * blk, (j - j0 + 1) * blk)
            w_blocks[j - j0], carry = block_weights(nz[:, cols], lkeep[:, cols], carry, j == i)
        w = w_blocks[0] if len(w_blocks) == 1 else jnp.concatenate(w_blocks, axis=1)
        acc = jnp.dot(w, v_ref[j0 * blk:(i + 1) * blk, lanes(hd)], preferred_element_type=F32)
        o_ref[rows(i), lanes(hd)] = acc.astype(o_ref.dtype)
        if j0 > 0:
            acc_ref[rows(i), lanes(hd)] = acc
            carry_ref[rows(i), lanes(hd)] = jnp.broadcast_to(carry, (blk, LANES))
            q32 = q_ref[rows(i), lanes(hd)].astype(F32)
            logit_bound[i, hd] = (jnp.sqrt(jnp.max(jnp.sum(q32 * q32, axis=1, keepdims=True)) * k_sq[hd])
                                  * 1.01 + 1.0)
            carry_max[i, hd] = jnp.max(carry)

    k_sq = {}
    if nq > SB_NEAR_KEY_BLOCKS:
        for hd in range(heads):
            k32 = k_ref[:, lanes(hd)].astype(F32)
            k_sq[hd] = jnp.max(jnp.sum(k32 * k32, axis=1, keepdims=True))
    logit_bound, carry_max = {}, {}

    order = [(i, hd) for i in range(nq - 1, -1, -1) for hd in range(heads)]
    nz_of, lk_of = {}, {}
    for step in range(len(order) + 2):
        if step < len(order):
            nz_of[order[step]] = logits(*order[step])
        if 0 <= step - 1 < len(order):
            lk_of[order[step - 1]] = log_keep(nz_of[order[step - 1]])
        if 0 <= step - 2 < len(order):
            unit = order[step - 2]
            finish(*unit, nz_of.pop(unit), lk_of.pop(unit))

    far_units = [(i, hd) for hd in range(heads) for i in range(SB_NEAR_KEY_BLOCKS, nq)]

    def far_part():
        for i, hd in far_units:
            def more(state, i=i, hd=hd):
                j, c_max = state
                return jnp.logical_and(j >= 0, c_max + logit_bound[i, hd] >= SB_ZERO_WEIGHT_LOG2)

            def take_block(state, i=i, hd=hd):
                j, _ = state
                keys = pl.ds(pl.multiple_of(j * blk, blk), blk)
                nz = lax.dot_general(q_ref[rows(i), lanes(hd)], k_ref[keys, lanes(hd)], (((1,), (1,)), ((), ())),
                                     preferred_element_type=F32)
                first_lane = slice(hd * HEAD_DIM, hd * HEAD_DIM + 1)
                w, carry = block_weights(nz, log_keep(nz), carry_ref[rows(i), first_lane], False)
                acc_ref[rows(i), lanes(hd)] += jnp.dot(w, v_ref[keys, lanes(hd)], preferred_element_type=F32)
                carry_ref[rows(i), lanes(hd)] = jnp.broadcast_to(carry, (blk, LANES))
                return j - 1, jnp.max(carry)

            j_end, _ = lax.while_loop(more, take_block, (jnp.int32(i - SB_NEAR_KEY_BLOCKS), carry_max[i, hd]))

            @pl.when(j_end < i - SB_NEAR_KEY_BLOCKS)
            def _(i=i, hd=hd):
                o_ref[rows(i), lanes(hd)] = acc_ref[rows(i), lanes(hd)].astype(o_ref.dtype)

    if far_units:
        needed = functools.reduce(jnp.maximum, [carry_max[u] + logit_bound[u] for u in far_units])
        pl.when(needed >= SB_ZERO_WEIGHT_LOG2)(far_part)


def _sb_attention(proj, n_heads, *, blk, heads_per_step):
    b, s, _ = proj.shape
    blk = _tile(s, blk)
    tri = (jnp.arange(blk)[:, None] >= jnp.arange(blk)[None, :]).astype(BF16)
    tri2 = jnp.concatenate([tri, tri], axis=0)
    groups, width = n_heads // _tile(n_heads, heads_per_step), _tile(n_heads, heads_per_step) * HEAD_DIM
    return pl.pallas_call(
        functools.partial(_sb_attn_kernel, blk=blk),
        out_shape=jax.ShapeDtypeStruct((b, s, n_heads * HEAD_DIM), BF16),
        grid=(b, groups),
        in_specs=[
            pl.BlockSpec((None, s, width), lambda bi, h: (bi, 0, h)),
            pl.BlockSpec((None, s, width), lambda bi, h: (bi, 0, groups + h)),
            pl.BlockSpec((None, s, width), lambda bi, h: (bi, 0, 2 * groups + h)),
            pl.BlockSpec((2 * blk, blk), lambda bi, h: (0, 0)),
        ],
        out_specs=pl.BlockSpec((None, s, width), lambda bi, h: (bi, 0, h)),
        scratch_shapes=[pltpu.VMEM((s, width), F32), pltpu.VMEM((s, width), F32)],
        compiler_params=_params("parallel", "parallel"),
        name="sb_attention",
    )(proj, proj, proj, tri2)


def _mla_attn_kernel(q_ref, k_ref, v_ref, o_ref, *, blk, scale):
    for hd in range(v_ref.shape[1] // MLA_V_DIM):
        qk_cols = slice(hd * MLA_QK_PAD, (hd + 1) * MLA_QK_PAD)
        v_cols = slice(hd * MLA_V_DIM, (hd + 1) * MLA_V_DIM)
        _mla_attn_head(q_ref.at[:, qk_cols], k_ref.at[:, qk_cols], v_ref.at[:, v_cols], o_ref.at[:, v_cols],
                       blk=blk, scale=scale)


def _mla_attn_head(q_ref, k_ref, v_ref, o_ref, *, blk, scale):
    row = lax.broadcasted_iota(jnp.int32, (blk, blk), 0)
    col = lax.broadcasted_iota(jnp.int32, (blk, blk), 1)
    causal = col <= row
    nq = q_ref.shape[0] // blk

    def scores(i):
        q = q_ref[i * blk:(i + 1) * blk, :]
        return lax.dot_general(q, k_ref[0:(i + 1) * blk, :], (((1,), (1,)), ((), ())),
                               preferred_element_type=F32)

    def probs(i, s):
        diag = jnp.where(causal, s[:, i * blk:], -jnp.inf)
        s = diag if i == 0 else jnp.concatenate([s[:, :i * blk], diag], axis=1)
        m = jnp.max(s, axis=1, keepdims=True)
        p = jnp.exp2((s - m) * (scale * LOG2_E))
        return p.astype(BF16), jnp.sum(p, axis=1, keepdims=True)

    def finish(i, p, l):
        o = jnp.dot(p, v_ref[0:(i + 1) * blk, :], preferred_element_type=F32)
        o_ref[i * blk:(i + 1) * blk, :] = (o / l).astype(o_ref.dtype)

    order = list(range(nq - 1, -1, -1))
    s_of, p_of = {}, {}
    for step in range(nq + 2):
        if step < nq:
            s_of[order[step]] = scores(order[step])
        if 0 <= step - 1 < nq:
            i = order[step - 1]
            p_of[i] = probs(i, s_of.pop(i))
        if 0 <= step - 2 < nq:
            i = order[step - 2]
            finish(i, *p_of.pop(i))


def _mla_attention(q_cat, k_cat, v, n_heads, *, blk, heads_per_step):
    b, s, _ = q_cat.shape
    blk = _tile(s, blk)
    scale = (MLA_NOPE_DIM + MLA_ROPE_DIM) ** -0.5
    per_step = _tile(n_heads, heads_per_step)
    return pl.pallas_call(
        functools.partial(_mla_attn_kernel, blk=blk, scale=scale),
        out_shape=jax.ShapeDtypeStruct((b, s, n_heads * MLA_V_DIM), BF16),
        grid=(b, n_heads // per_step),
        in_specs=[
            pl.BlockSpec((None, s, per_step * MLA_QK_PAD), lambda bi, h: (bi, 0, h)),
            pl.BlockSpec((None, s, per_step * MLA_QK_PAD), lambda bi, h: (bi, 0, h)),
            pl.BlockSpec((None, s, per_step * MLA_V_DIM), lambda bi, h: (bi, 0, h)),
        ],
        out_specs=pl.BlockSpec((None, s, per_step * MLA_V_DIM), lambda bi, h: (bi, 0, h)),
        compiler_params=_params("parallel", "parallel"),
        name="mla_attention",
    )(q_cat, k_cat, v)


def _latent_kv_kernel(h_ref, g_ref, wdkv_ref, gl_ref, wukv_ref, ca_ref, cb_ref,
                      kcat_ref, v_ref, *, n_heads, rank):
    rows = h_ref.shape[0] // ROW_SPLITS
    for c in range(ROW_SPLITS):
        r = slice(c * rows, (c + 1) * rows)
        xn = _rms(h_ref[r, :], g_ref[...]).astype(BF16)
        ckv = jnp.dot(xn, wdkv_ref[...], preferred_element_type=F32)
        c_latent = _rms(ckv[:, :rank], gl_ref[...]).astype(BF16)
        k_rope = _rope(ckv[:, rank:], ca_ref[r, :], cb_ref[r, :]).astype(BF16)
        for h in range(n_heads):
            kv = jnp.dot(c_latent, wukv_ref[:, h * 256:(h + 1) * 256], preferred_element_type=F32)
            kcat_ref[r, h * MLA_QK_PAD:h * MLA_QK_PAD + MLA_NOPE_DIM] = kv[:, :MLA_NOPE_DIM].astype(BF16)
            kcat_ref[r, h * MLA_QK_PAD + MLA_NOPE_DIM:(h + 1) * MLA_QK_PAD] = k_rope
            v_ref[r, h * MLA_V_DIM:(h + 1) * MLA_V_DIM] = kv[:, MLA_NOPE_DIM:].astype(BF16)


def _latent_kv(h, g, w_dkv_pad, g_latent, w_ukv, ca, cb, n_heads, *, tm):
    m, d = h.shape
    rank = g_latent.shape[0]
    tm = _tile(m, tm)
    row = lambda i: (i, 0)
    fixed = lambda i: (0, 0)
    return pl.pallas_call(
        functools.partial(_latent_kv_kernel, n_heads=n_heads, rank=rank),
        out_shape=(jax.ShapeDtypeStruct((m, n_heads * MLA_QK_PAD), BF16),
                   jax.ShapeDtypeStruct((m, n_heads * MLA_V_DIM), BF16)),
        grid=(m // tm,),
        in_specs=[
            pl.BlockSpec((tm, d), row),
            pl.BlockSpec((1, d), fixed),
            pl.BlockSpec(w_dkv_pad.shape, fixed),
            pl.BlockSpec((1, rank), fixed),
            pl.BlockSpec(w_ukv.shape, fixed),
            pl.BlockSpec((tm, LANES), row),
            pl.BlockSpec((tm, LANES), row),
        ],
        out_specs=(pl.BlockSpec((tm, n_heads * MLA_QK_PAD), row),
                   pl.BlockSpec((tm, n_heads * MLA_V_DIM), row)),
        compiler_params=_params("parallel"),
        name="latent_kv",
    )(h, g.reshape(1, d), w_dkv_pad, g_latent.reshape(1, rank), w_ukv, ca, cb)


def _mla_in_kernel(h_ref, g_ref, win_ref, gq_ref, wuq_ref, ca_ref, cb_ref,
                   qcat_ref, memq_ref, *, n_heads, rank):
    rows = h_ref.shape[0] // ROW_SPLITS
    for c in range(ROW_SPLITS):
        r = slice(c * rows, (c + 1) * rows)
        xn = _rms(h_ref[r, :], g_ref[...]).astype(BF16)
        proj = jnp.dot(xn, win_ref[...], preferred_element_type=F32)
        memq_ref[r, :] = proj[:, rank:].astype(BF16)
        c_q = _rms(proj[:, :rank], gq_ref[...]).astype(BF16)
        ca, cb = ca_ref[r, :], cb_ref[r, :]
        for h in range(n_heads):
            q = jnp.dot(c_q, wuq_ref[:, h * MLA_QK_PAD:(h + 1) * MLA_QK_PAD], preferred_element_type=F32)
            qcat_ref[r, h * MLA_QK_PAD:h * MLA_QK_PAD + MLA_NOPE_DIM] = q[:, :MLA_NOPE_DIM].astype(BF16)
            qcat_ref[r, h * MLA_QK_PAD + MLA_NOPE_DIM:(h + 1) * MLA_QK_PAD] = (
                _rope(q[:, MLA_NOPE_DIM:], ca, cb).astype(BF16))


def _mla_in(h, g, w_in, g_q, w_uq_pad, layer, ca, cb, n_heads, *, tm):
    m, d = h.shape
    rank = g_q.shape[0]
    n_memq = w_in.shape[2] - rank
    tm = _tile(m, tm)
    row = lambda i: (i, 0)
    fixed = lambda i: (0, 0)
    stacked = lambda i: (layer, 0, 0)
    return pl.pallas_call(
        functools.partial(_mla_in_kernel, n_heads=n_heads, rank=rank),
        out_shape=(jax.ShapeDtypeStruct((m, n_heads * MLA_QK_PAD), BF16),
                   jax.ShapeDtypeStruct((m, n_memq), BF16)),
        grid=(m // tm,),
        in_specs=[
            pl.BlockSpec((tm, d), row),
            pl.BlockSpec((1, d), fixed),
            pl.BlockSpec((None,) + w_in.shape[1:], stacked),
            pl.BlockSpec((1, rank), fixed),
            pl.BlockSpec((None,) + w_uq_pad.shape[1:], stacked),
            pl.BlockSpec((tm, LANES), row),
            pl.BlockSpec((tm, LANES), row),
        ],
        out_specs=(pl.BlockSpec((tm, n_heads * MLA_QK_PAD), row),
                   pl.BlockSpec((tm, n_memq), row)),
        compiler_params=_params("parallel"),
        name="mla_in",
    )(h, g.reshape(1, d), w_in, g_q.reshape(1, rank), w_uq_pad, ca, cb)


def _rope_tables(positions):
    half = MLA_ROPE_DIM // 2
    inv_freq = ROPE_THETA ** (-jnp.arange(half, dtype=F32) / half)
    ang = positions.reshape(-1).astype(F32)[:, None] * inv_freq
    cos, sin = jnp.cos(ang), jnp.sin(ang)
    zeros = jnp.zeros((ang.shape[0], LANES - MLA_ROPE_DIM), F32)
    ca = jnp.concatenate([cos, cos, zeros], axis=-1)
    cb = jnp.concatenate([-sin, sin, zeros], axis=-1)
    return ca, cb


def kernel(x, mem, positions, attn_norm_g, ffn_norm_g, a_w_in, a_w_out, b_w_in, b_q_norm_g, b_w_uq, b_w_out, mem_norm_g, w_mem_kv, kv_norm_g, w_dkv, kv_latent_g, w_ukv, ffn_w_gu, ffn_w_down, final_norm_g):
    b, s, d = x.shape
    mem_len = mem.shape[1]
    depth = attn_norm_g.shape[0]
    n_a = a_w_in.shape[0]
    n_b = b_w_in.shape[0]
    mq_w = MEM_HEADS * HEAD_DIM
    sb_heads = (a_w_in.shape[2] - mq_w) // (3 * HEAD_DIM)
    q_rank = b_q_norm_g.shape[1]
    mla_heads = b_w_uq.shape[2] // (MLA_NOPE_DIM + MLA_ROPE_DIM)

    a_w_in_b = a_w_in.astype(BF16)
    a_w_out_b = a_w_out.astype(BF16)
    b_w_in_b = b_w_in.astype(BF16)
    b_w_out_b = b_w_out.astype(BF16)
    w_ukv_b = w_ukv.astype(BF16)
    w_mem_all = jnp.transpose(w_mem_kv, (1, 0, 2)).reshape(d, depth * 2 * mq_w).astype(BF16)
    w_dkv_pad = jnp.pad(w_dkv, ((0, 0), (0, LANES - MLA_ROPE_DIM))).astype(BF16)
    w_uq_pad = jnp.pad(
        b_w_uq.reshape(n_b, q_rank, mla_heads, MLA_NOPE_DIM + MLA_ROPE_DIM),
        ((0, 0), (0, 0), (0, 0), (0, MLA_QK_PAD - MLA_NOPE_DIM - MLA_ROPE_DIM)),
    ).reshape(n_b, q_rank, mla_heads * MLA_QK_PAD).astype(BF16)
    ca, cb = _rope_tables(positions)
    a_col_scale = jnp.concatenate([jnp.full((sb_heads * HEAD_DIM,), SB_Q_PRESCALE, F32),
                                   jnp.ones((a_w_in.shape[2] - sb_heads * HEAD_DIM,), F32)])

    h = x.reshape(b * s, d)
    mkv = _norm_matmul(mem.reshape(b * mem_len, d), mem_norm_g, w_mem_all[None], 0,
                       jnp.ones((w_mem_all.shape[1],), F32), tm=NORM_MATMUL_ROWS, tn=NORM_MATMUL_COLS)
    mkv = mkv.reshape(b, mem_len, depth * 2 * mq_w)

    k_cat = v_lat = None
    for layer in range(depth):
        if layer == n_a:
            k_cat, v_lat = _latent_kv(h, kv_norm_g, w_dkv_pad, kv_latent_g, w_ukv_b, ca, cb,
                                      mla_heads, tm=LATENT_KV_ROWS)
            k_cat = k_cat.reshape(b, s, -1)
            v_lat = v_lat.reshape(b, s, -1)
        if layer < n_a:
            proj = _norm_matmul(h, attn_norm_g[layer], a_w_in_b, layer, a_col_scale,
                                tm=NORM_MATMUL_ROWS, tn=NORM_MATMUL_COLS)
            mix = _sb_attention(proj.reshape(b, s, -1), sb_heads, blk=ATTENTION_BLOCK,
                                heads_per_step=SB_HEADS_PER_STEP)
            q_src, q_block = proj, (3 * sb_heads * HEAD_DIM) // mq_w
            w_out, li = a_w_out_b, layer
        else:
            li = layer - n_a
            q_cat, mem_q = _mla_in(h, attn_norm_g[layer], b_w_in_b, b_q_norm_g[li], w_uq_pad, li,
                                   ca, cb, mla_heads, tm=MLA_IN_ROWS)
            mix = _mla_attention(q_cat.reshape(b, s, -1), k_cat, v_lat, mla_heads, blk=ATTENTION_BLOCK,
                                 heads_per_step=MLA_HEADS_PER_STEP)
            q_src, q_block = mem_q, 0
            w_out = b_w_out_b
        h, w_gu_b, w_d_b = _mem_out_proj(mix.reshape(b * s, -1), q_src, q_block, mkv, 2 * layer, w_out, li, h,
                                         ffn_w_gu, ffn_w_down, layer, tm=MEM_OUT_PROJ_ROWS)
        h = _ffn(h, ffn_norm_g[layer], w_gu_b[None], w_d_b[None], 0, final_norm_g,
                 final_norm=(layer == depth - 1), tm=FFN_ROWS, tf=FFN_HIDDEN)
    return h.reshape(b, s, d)
```

```python
import functools

import jax
import jax.numpy as jnp
from jax import lax
from jax.experimental import pallas as pl
from jax.experimental.pallas import tpu as pltpu

HEAD_DIM = 128
MEM_HEADS = 4
MLA_NOPE_DIM = 128
MLA_ROPE_DIM = 64
MLA_V_DIM = 128
ROPE_THETA = 10000.0
RMS_EPS = 1e-6
LOG2_E = 1.4426950408889634
SB_Q_PRESCALE = -(HEAD_DIM ** -0.5) * LOG2_E
SB_NEAR_KEY_BLOCKS = 2
SB_ZERO_WEIGHT_LOG2 = -140.0

LANES = 128
MLA_QK_PAD = 2 * LANES
NORM_CHUNK_ROWS = 256
FFN_FIRST_STEP_SPLITS = 2
NORM_MATMUL_FIRST_STEP_SPLITS = 2
ROW_SPLITS = 2

VMEM_LIMIT_BYTES = 48 * 1024 * 1024
VMEM_LIMIT_BYTES_LARGE = 58 * 1024 * 1024
NORM_MATMUL_ROWS, NORM_MATMUL_COLS = 512, 1024
FFN_ROWS, FFN_HIDDEN = 1024, 512
MEM_OUT_PROJ_ROWS = 512
MLA_IN_ROWS = 1024
LATENT_KV_ROWS = 1024
ATTENTION_BLOCK = 256
SB_HEADS_PER_STEP = 2
MLA_HEADS_PER_STEP = 4

BF16 = jnp.bfloat16
F32 = jnp.float32


def _params(*semantics, vmem_limit_bytes=VMEM_LIMIT_BYTES, fuse_inputs=None):
    return pltpu.CompilerParams(dimension_semantics=semantics, vmem_limit_bytes=vmem_limit_bytes,
                                allow_input_fusion=fuse_inputs)


def _tile(n, pref):
    t = min(n, pref)
    assert n % t == 0, (n, pref)
    return t


def _rms(x, g):
    return (x * lax.rsqrt(jnp.mean(x * x, axis=-1, keepdims=True) + RMS_EPS)) * g


def _rmsnorm_rows(x_ref, g_ref, out_ref):
    rows = x_ref.shape[0]
    chunk = min(rows, NORM_CHUNK_ROWS)
    g = g_ref[...]

    def body(c, carry):
        r = pl.multiple_of(c * chunk, chunk)
        out_ref[pl.ds(r, chunk), :] = _rms(x_ref[pl.ds(r, chunk), :], g).astype(out_ref.dtype)
        return carry

    lax.fori_loop(0, rows // chunk, body, 0)


def _rope(x, ca, cb):
    partner = pltpu.roll(x, 96, 1) + pltpu.roll(x, 32, 1)
    return x * ca + partner * cb


def _norm_matmul_kernel(h_ref, g_ref, w_ref, cs_ref, o_ref, xn_ref, *, tn):
    def column_tile(xn, j):
        cols = slice(j * tn, (j + 1) * tn)
        return (jnp.dot(xn, w_ref[:, cols], preferred_element_type=F32) * cs_ref[:, cols]).astype(o_ref.dtype)

    rows = h_ref.shape[0] // NORM_MATMUL_FIRST_STEP_SPLITS
    g = g_ref[...]
    for c in range(NORM_MATMUL_FIRST_STEP_SPLITS):
        r = slice(c * rows, (c + 1) * rows)
        xn = _rms(h_ref[r, :], g).astype(BF16)
        xn_ref[r, :] = xn
        o_ref[r, 0:tn] = column_tile(xn, 0)
    for j in range(1, o_ref.shape[1] // tn):
        o_ref[:, j * tn:(j + 1) * tn] = column_tile(xn_ref[...], j)


def _norm_matmul(h, g, w, layer, col_scale, *, tm, tn):
    m, d = h.shape
    n = w.shape[2]
    tm, tn = _tile(m, tm), _tile(n, tn)
    return pl.pallas_call(
        functools.partial(_norm_matmul_kernel, tn=tn),
        out_shape=jax.ShapeDtypeStruct((m, n), BF16),
        grid=(m // tm,),
        in_specs=[
            pl.BlockSpec((tm, d), lambda i: (i, 0)),
            pl.BlockSpec((1, d), lambda i: (0, 0)),
            pl.BlockSpec((None, d, n), lambda i: (layer, 0, 0), pipeline_mode=pl.Buffered(1)),
            pl.BlockSpec((1, n), lambda i: (0, 0)),
        ],
        out_specs=pl.BlockSpec((tm, n), lambda i: (i, 0)),
        scratch_shapes=[pltpu.VMEM((tm, d), BF16)],
        compiler_params=_params("parallel", vmem_limit_bytes=VMEM_LIMIT_BYTES_LARGE),
        name="norm_matmul",
    )(h, g.reshape(1, d), w, col_scale.reshape(1, n))


def _ffn_kernel(h_ref, g_ref, wg_ref, wu_ref, wd_ref, gf_ref, o_ref, xn_ref, *, final_norm):
    j = pl.program_id(1)

    def hidden_tile(xn):
        gate = jnp.dot(xn, wg_ref[...], preferred_element_type=F32)
        up = jnp.dot(xn, wu_ref[...], preferred_element_type=F32)
        act = (gate * jax.nn.sigmoid(gate) * up).astype(BF16)
        return jnp.dot(act, wd_ref[...], preferred_element_type=F32)

    @pl.when(j == 0)
    def _():
        half = h_ref.shape[0] // FFN_FIRST_STEP_SPLITS
        g = g_ref[...]
        for c in range(FFN_FIRST_STEP_SPLITS):
            h = h_ref[c * half:(c + 1) * half, :]
            xn = _rms(h, g).astype(BF16)
            xn_ref[c * half:(c + 1) * half, :] = xn
            o_ref[c * half:(c + 1) * half, :] = h + hidden_tile(xn)

    @pl.when(j > 0)
    def _():
        o_ref[...] += hidden_tile(xn_ref[...])

    if final_norm:
        @pl.when(j == pl.num_programs(1) - 1)
        def _():
            _rmsnorm_rows(o_ref, gf_ref, o_ref)


def _ffn(h, g, w_gu, w_d, layer, gf, *, final_norm, tm, tf):
    m, d = h.shape
    f = w_d.shape[1]
    tm, tf = _tile(m, tm), _tile(f, tf)
    nf = f // tf
    return pl.pallas_call(
        functools.partial(_ffn_kernel, final_norm=final_norm),
        out_shape=jax.ShapeDtypeStruct((m, d), F32),
        grid=(m // tm, nf),
        in_specs=[
            pl.BlockSpec((tm, d), lambda i, j: (i, 0)),
            pl.BlockSpec((1, d), lambda i, j: (0, 0)),
            pl.BlockSpec((None, d, tf), lambda i, j: (layer, 0, j)),
            pl.BlockSpec((None, d, tf), lambda i, j: (layer, 0, nf + j)),
            pl.BlockSpec((None, tf, d), lambda i, j: (layer, j, 0)),
            pl.BlockSpec((1, d), lambda i, j: (0, 0)),
        ],
        out_specs=pl.BlockSpec((tm, d), lambda i, j: (i, 0)),
        scratch_shapes=[pltpu.VMEM((tm, d), BF16)],
        compiler_params=_params("parallel", "arbitrary", vmem_limit_bytes=VMEM_LIMIT_BYTES_LARGE),
        name="ffn",
    )(h, g.reshape(1, d), w_gu, w_gu, w_d, gf.reshape(1, d))


def _mem_out_proj_kernel(mix_ref, qm_ref, mk_ref, mv_ref, w1_ref, w2_ref, h_ref, gu_ref, wd_ref,
                         o_ref, gu_b_ref, wd_b_ref, *, scale):
    gu_b_ref[...] = gu_ref[...].astype(BF16)
    wd_b_ref[...] = wd_ref[...].astype(BF16)
    acc = h_ref[...] + jnp.dot(mix_ref[...], w1_ref[...], preferred_element_type=F32)
    heads = []
    for hd in range(MEM_HEADS):
        cols = slice(hd * HEAD_DIM, (hd + 1) * HEAD_DIM)
        s = lax.dot_general(qm_ref[:, cols], mk_ref[:, cols], (((1,), (1,)), ((), ())),
                            preferred_element_type=F32)
        m = jnp.max(s, axis=1, keepdims=True)
        p = jnp.exp2((s - m) * (scale * LOG2_E))
        l = jnp.sum(p, axis=1, keepdims=True)
        o = jnp.dot(p.astype(BF16), mv_ref[:, cols], preferred_element_type=F32)
        heads.append((o / l).astype(BF16))
    mem_out = jnp.concatenate(heads, axis=1)
    o_ref[...] = acc + jnp.dot(mem_out, w2_ref[...], preferred_element_type=F32)


def _cast_rows(total_rows, steps):
    for rows in range(16, total_rows + 1, 16):
        if total_rows % rows == 0 and total_rows // rows <= steps:
            return rows
    raise ValueError((total_rows, steps))


def _mem_out_proj(mix, q_src, q_block, mkv, kv_block, w_out, layer, h, ffn_w_gu, ffn_w_d, ffn_layer, *, tm):
    m, d = h.shape
    bsz, ml, _ = mkv.shape
    k1, k2 = mix.shape[1], MEM_HEADS * HEAD_DIM
    assert k1 % k2 == 0 and w_out.shape[1] == k1 + k2
    tm = _tile(m // bsz, tm)
    steps_per_batch = (m // bsz) // tm
    steps = m // tm
    gu_rows, wd_rows = _cast_rows(ffn_w_gu.shape[1], steps), _cast_rows(ffn_w_d.shape[1], steps)
    gu_last, wd_last = ffn_w_gu.shape[1] // gu_rows - 1, ffn_w_d.shape[1] // wd_rows - 1
    return pl.pallas_call(
        functools.partial(_mem_out_proj_kernel, scale=HEAD_DIM ** -0.5),
        out_shape=(jax.ShapeDtypeStruct((m, d), F32),
                   jax.ShapeDtypeStruct(ffn_w_gu.shape[1:], BF16),
                   jax.ShapeDtypeStruct(ffn_w_d.shape[1:], BF16)),
        grid=(steps,),
        in_specs=[
            pl.BlockSpec((tm, k1), lambda i: (i, 0)),
            pl.BlockSpec((tm, k2), lambda i: (i, q_block)),
            pl.BlockSpec((None, ml, k2), lambda i: (i // steps_per_batch, 0, kv_block)),
            pl.BlockSpec((None, ml, k2), lambda i: (i // steps_per_batch, 0, kv_block + 1)),
            pl.BlockSpec((None, k1, d), lambda i: (layer, 0, 0)),
            pl.BlockSpec((None, k2, d), lambda i: (layer, k1 // k2, 0)),
            pl.BlockSpec((tm, d), lambda i: (i, 0)),
            pl.BlockSpec((None, gu_rows, ffn_w_gu.shape[2]), lambda i: (ffn_layer, jnp.minimum(i, gu_last), 0)),
            pl.BlockSpec((None, wd_rows, ffn_w_d.shape[2]), lambda i: (ffn_layer, jnp.minimum(i, wd_last), 0)),
        ],
        out_specs=(pl.BlockSpec((tm, d), lambda i: (i, 0)),
                   pl.BlockSpec((gu_rows, ffn_w_gu.shape[2]), lambda i: (jnp.minimum(i, gu_last), 0)),
                   pl.BlockSpec((wd_rows, ffn_w_d.shape[2]), lambda i: (jnp.minimum(i, wd_last), 0))),
        compiler_params=_params("arbitrary"),
        name="mem_out_proj",
    )(mix, q_src, mkv, mkv, w_out, w_out, h, ffn_w_gu, ffn_w_d)


def _sb_attn_kernel(q_ref, k_ref, v_ref, tri_ref, o_ref, acc_ref, carry_ref, *, blk):
    tri2 = tri_ref[...]
    row = lax.broadcasted_iota(jnp.int32, (blk, blk), 0)
    col = lax.broadcasted_iota(jnp.int32, (blk, blk), 1)
    strictly_past = col < row
    nq = q_ref.shape[0] // blk
    heads = q_ref.shape[1] // HEAD_DIM

    def rows(i):
        return slice(i * blk, (i + 1) * blk)

    def lanes(hd):
        return slice(hd * HEAD_DIM, (hd + 1) * HEAD_DIM)

    def log_keep(nz):
        neg_abs = lax.bitcast_convert_type(
            lax.bitcast_convert_type(nz, jnp.uint32) | jnp.uint32(0x80000000), F32)
        return jnp.minimum(nz, 0.0) - jnp.log2(1.0 + jnp.exp2(neg_abs))

    def block_weights(nz, lk, carry, diagonal):
        if diagonal:
            lk = jnp.where(strictly_past, lk, 0.0)
        hi = lk.astype(BF16)
        lo = (lk - hi.astype(F32)).astype(BF16)
        log_w = jnp.dot(jnp.concatenate([hi, lo], axis=1), tri2, preferred_element_type=F32) - nz
        if carry is not None:
            log_w = log_w + carry
        w = jnp.exp2(log_w)
        if diagonal:
            w = jnp.where(strictly_past, w, 0.0)
        block_sum = jnp.sum(lk, axis=1, keepdims=True)
        return w.astype(BF16), (block_sum if carry is None else carry + block_sum)

    def first_key_block(i):
        return max(i - SB_NEAR_KEY_BLOCKS + 1, 0)

    def logits(i, hd):
        keys = slice(first_key_block(i) * blk, (i + 1) * blk)
        return lax.dot_general(q_ref[rows(i), lanes(hd)], k_ref[keys, lanes(hd)], (((1,), (1,)), ((), ())),
                               preferred_element_type=F32)

    def finish(i, hd, nz, lkeep):
        j0 = first_key_block(i)
        w_blocks, carry = [None] * (i + 1 - j0), None
        for j in range(i, j0 - 1, -1):
            cols = slice((j - j0) * blk, (j - j0 + 1) * blk)
            w_blocks[j - j0], carry = block_weights(nz[:, cols], lkeep[:, cols], carry, j == i)
        w = w_blocks[0] if len(w_blocks) == 1 else jnp.concatenate(w_blocks, axis=1)
        acc = jnp.dot(w, v_ref[j0 * blk:(i + 1) * blk, lanes(hd)], preferred_element_type=F32)
        o_ref[rows(i), lanes(hd)] = acc.astype(o_ref.dtype)
        if j0 > 0:
            acc_ref[rows(i), lanes(hd)] = acc
            carry_ref[rows(i), lanes(hd)] = jnp.broadcast_to(carry, (blk, LANES))
            q32 = q_ref[rows(i), lanes(hd)].astype(F32)
            logit_bound[i, hd] = (jnp.sqrt(jnp.max(jnp.sum(q32 * q32, axis=1, keepdims=True)) * k_sq[hd])
                                  * 1.01 + 1.0)
            carry_max[i, hd] = jnp.max(carry)

    k_sq = {}
    if nq > SB_NEAR_KEY_BLOCKS:
        for hd in range(heads):
            k32 = k_ref[:, lanes(hd)].astype(F32)
            k_sq[hd] = jnp.max(jnp.sum(k32 * k32, axis=1, keepdims=True))
    logit_bound, carry_max = {}, {}

    order = [(i, hd) for i in range(nq - 1, -1, -1) for hd in range(heads)]
    nz_of, lk_of = {}, {}
    for step in range(len(order) + 2):
        if step < len(order):
            nz_of[order[step]] = logits(*order[step])
        if 0 <= step - 1 < len(order):
            lk_of[order[step - 1]] = log_keep(nz_of[order[step - 1]])
        if 0 <= step - 2 < len(order):
            unit = order[step - 2]
            finish(*unit, nz_of.pop(unit), lk_of.pop(unit))

    far_units = [(i, hd) for hd in range(heads) for i in range(SB_NEAR_KEY_BLOCKS, nq)]

    def far_part():
        for i, hd in far_units:
            def more(state, i=i, hd=hd):
                j, c_max = state
                return jnp.logical_and(j >= 0, c_max + logit_bound[i, hd] >= SB_ZERO_WEIGHT_LOG2)

            def take_block(state, i=i, hd=hd):
                j, _ = state
                keys = pl.ds(pl.multiple_of(j * blk, blk), blk)
                nz = lax.dot_general(q_ref[rows(i), lanes(hd)], k_ref[keys, lanes(hd)], (((1,), (1,)), ((), ())),
                                     preferred_element_type=F32)
                first_lane = slice(hd * HEAD_DIM, hd * HEAD_DIM + 1)
                w, carry = block_weights(nz, log_keep(nz), carry_ref[rows(i), first_lane], False)
                acc_ref[rows(i), lanes(hd)] += jnp.dot(w, v_ref[keys, lanes(hd)], preferred_element_type=F32)
                carry_ref[rows(i), lanes(hd)] = jnp.broadcast_to(carry, (blk, LANES))
                return j - 1, jnp.max(carry)

            j_end, _ = lax.while_loop(more, take_block, (jnp.int32(i - SB_NEAR_KEY_BLOCKS), carry_max[i, hd]))

            @pl.when(j_end < i - SB_NEAR_KEY_BLOCKS)
            def _(i=i, hd=hd):
                o_ref[rows(i), lanes(hd)] = acc_ref[rows(i), lanes(hd)].astype(o_ref.dtype)

    if far_units:
        needed = functools.reduce(jnp.maximum, [carry_max[u] + logit_bound[u] for u in far_units])
        pl.when(needed >= SB_ZERO_WEIGHT_LOG2)(far_part)


def _sb_attention(proj, n_heads, *, blk, heads_per_step):
    b, s, _ = proj.shape
    blk = _tile(s, blk)
    tri = (jnp.arange(blk)[:, None] >= jnp.arange(blk)[None, :]).astype(BF16)
    tri2 = jnp.concatenate([tri, tri], axis=0)
    groups, width = n_heads // _tile(n_heads, heads_per_step), _tile(n_heads, heads_per_step) * HEAD_DIM
    return pl.pallas_call(
        functools.partial(_sb_attn_kernel, blk=blk),
        out_shape=jax.ShapeDtypeStruct((b, s, n_heads * HEAD_DIM), BF16),
        grid=(b, groups),
        in_specs=[
            pl.BlockSpec((None, s, width), lambda bi, h: (bi, 0, h)),
            pl.BlockSpec((None, s, width), lambda bi, h: (bi, 0, groups + h)),
            pl.BlockSpec((None, s, width), lambda bi, h: (bi, 0, 2 * groups + h)),
            pl.BlockSpec((2 * blk, blk), lambda bi, h: (0, 0)),
        ],
        out_specs=pl.BlockSpec((None, s, width), lambda bi, h: (bi, 0, h)),
        scratch_shapes=[pltpu.VMEM((s, width), F32), pltpu.VMEM((s, width), F32)],
        compiler_params=_params("parallel", "parallel"),
        name="sb_attention",
    )(proj, proj, proj, tri2)


def _mla_attn_kernel(q_ref, k_ref, v_ref, o_ref, *, blk, scale):
    for hd in range(v_ref.shape[1] // MLA_V_DIM):
        qk_cols = slice(hd * MLA_QK_PAD, (hd + 1) * MLA_QK_PAD)
        v_cols = slice(hd * MLA_V_DIM, (hd + 1) * MLA_V_DIM)
        _mla_attn_head(q_ref.at[:, qk_cols], k_ref.at[:, qk_cols], v_ref.at[:, v_cols], o_ref.at[:, v_cols],
                       blk=blk, scale=scale)


def _mla_attn_head(q_ref, k_ref, v_ref, o_ref, *, blk, scale):
    row = lax.broadcasted_iota(jnp.int32, (blk, blk), 0)
    col = lax.broadcasted_iota(jnp.int32, (blk, blk), 1)
    causal = col <= row
    nq = q_ref.shape[0] // blk

    def scores(i):
        q = q_ref[i * blk:(i + 1) * blk, :]
        return lax.dot_general(q, k_ref[0:(i + 1) * blk, :], (((1,), (1,)), ((), ())),
                               preferred_element_type=F32)

    def probs(i, s):
        diag = jnp.where(causal, s[:, i * blk:], -jnp.inf)
        s = diag if i == 0 else jnp.concatenate([s[:, :i * blk], diag], axis=1)
        m = jnp.max(s, axis=1, keepdims=True)
        p = jnp.exp2((s - m) * (scale * LOG2_E))
        return p.astype(BF16), jnp.sum(p, axis=1, keepdims=True)

    def finish(i, p, l):
        o = jnp.dot(p, v_ref[0:(i + 1) * blk, :], preferred_element_type=F32)
        o_ref[i * blk:(i + 1) * blk, :] = (o / l).astype(o_ref.dtype)

    order = list(range(nq - 1, -1, -1))
    s_of, p_of = {}, {}
    for step in range(nq + 2):
        if step < nq:
            s_of[order[step]] = scores(order[step])
        if 0 <= step - 1 < nq:
            i = order[step - 1]
            p_of[i] = probs(i, s_of.pop(i))
        if 0 <= step - 2 < nq:
            i = order[step - 2]
            finish(i, *p_of.pop(i))


def _mla_attention(q_cat, k_cat, v, n_heads, *, blk, heads_per_step):
    b, s, _ = q_cat.shape
    blk = _tile(s, blk)
    scale = (MLA_NOPE_DIM + MLA_ROPE_DIM) ** -0.5
    per_step = _tile(n_heads, heads_per_step)
    return pl.pallas_call(
        functools.partial(_mla_attn_kernel, blk=blk, scale=scale),
        out_shape=jax.ShapeDtypeStruct((b, s, n_heads * MLA_V_DIM), BF16),
        grid=(b, n_heads // per_step),
        in_specs=[
            pl.BlockSpec((None, s, per_step * MLA_QK_PAD), lambda bi, h: (bi, 0, h)),
            pl.BlockSpec((None, s, per_step * MLA_QK_PAD), lambda bi, h: (bi, 0, h)),
            pl.BlockSpec((None, s, per_step * MLA_V_DIM), lambda bi, h: (bi, 0, h)),
        ],
        out_specs=pl.BlockSpec((None, s, per_step * MLA_V_DIM), lambda bi, h: (bi, 0, h)),
        compiler_params=_params("parallel", "parallel"),
        name="mla_attention",
    )(q_cat, k_cat, v)


def _latent_kv_kernel(h_ref, g_ref, wdkv_ref, gl_ref, wukv_ref, ca_ref, cb_ref,
                      kcat_ref, v_ref, *, n_heads, rank):
    rows = h_ref.shape[0] // ROW_SPLITS
    for c in range(ROW_SPLITS):
        r = slice(c * rows, (c + 1) * rows)
        xn = _rms(h_ref[r, :], g_ref[...]).astype(BF16)
        ckv = jnp.dot(xn, wdkv_ref[...], preferred_element_type=F32)
        c_latent = _rms(ckv[:, :rank], gl_ref[...]).astype(BF16)
        k_rope = _rope(ckv[:, rank:], ca_ref[r, :], cb_ref[r, :]).astype(BF16)
        for h in range(n_heads):
            kv = jnp.dot(c_latent, wukv_ref[:, h * 256:(h + 1) * 256], preferred_element_type=F32)
            kcat_ref[r, h * MLA_QK_PAD:h * MLA_QK_PAD + MLA_NOPE_DIM] = kv[:, :MLA_NOPE_DIM].astype(BF16)
            kcat_ref[r, h * MLA_QK_PAD + MLA_NOPE_DIM:(h + 1) * MLA_QK_PAD] = k_rope
            v_ref[r, h * MLA_V_DIM:(h + 1) * MLA_V_DIM] = kv[:, MLA_NOPE_DIM:].astype(BF16)


def _latent_kv(h, g, w_dkv_pad, g_latent, w_ukv, ca, cb, n_heads, *, tm):
    m, d = h.shape
    rank = g_latent.shape[0]
    tm = _tile(m, tm)
    row = lambda i: (i, 0)
    fixed = lambda i: (0, 0)
    return pl.pallas_call(
        functools.partial(_latent_kv_kernel, n_heads=n_heads, rank=rank),
        out_shape=(jax.ShapeDtypeStruct((m, n_heads * MLA_QK_PAD), BF16),
                   jax.ShapeDtypeStruct((m, n_heads * MLA_V_DIM), BF16)),
        grid=(m // tm,),
        in_specs=[
            pl.BlockSpec((tm, d), row),
            pl.BlockSpec((1, d), fixed),
            pl.BlockSpec(w_dkv_pad.shape, fixed),
            pl.BlockSpec((1, rank), fixed),
            pl.BlockSpec(w_ukv.shape, fixed),
            pl.BlockSpec((tm, LANES), row),
            pl.BlockSpec((tm, LANES), row),
        ],
        out_specs=(pl.BlockSpec((tm, n_heads * MLA_QK_PAD), row),
                   pl.BlockSpec((tm, n_heads * MLA_V_DIM), row)),
        compiler_params=_params("parallel", fuse_inputs=[False, False, True, False, True, True, True]),
        name="latent_kv",
    )(h, g.reshape(1, d), w_dkv_pad, g_latent.reshape(1, rank), w_ukv, ca, cb)


def _mla_in_kernel(h_ref, g_ref, win_ref, gq_ref, wuq_ref, ca_ref, cb_ref,
                   qcat_ref, memq_ref, *, n_heads, rank):
    rows = h_ref.shape[0] // ROW_SPLITS
    for c in range(ROW_SPLITS):
        r = slice(c * rows, (c + 1) * rows)
        xn = _rms(h_ref[r, :], g_ref[...]).astype(BF16)
        proj = jnp.dot(xn, win_ref[...], preferred_element_type=F32)
        memq_ref[r, :] = proj[:, rank:].astype(BF16)
        c_q = _rms(proj[:, :rank], gq_ref[...]).astype(BF16)
        ca, cb = ca_ref[r, :], cb_ref[r, :]
        for h in range(n_heads):
            q = jnp.dot(c_q, wuq_ref[:, h * MLA_QK_PAD:(h + 1) * MLA_QK_PAD], preferred_element_type=F32)
            qcat_ref[r, h * MLA_QK_PAD:h * MLA_QK_PAD + MLA_NOPE_DIM] = q[:, :MLA_NOPE_DIM].astype(BF16)
            qcat_ref[r, h * MLA_QK_PAD + MLA_NOPE_DIM:(h + 1) * MLA_QK_PAD] = (
                _rope(q[:, MLA_NOPE_DIM:], ca, cb).astype(BF16))


def _mla_in(h, g, w_in, g_q, w_uq_pad, layer, ca, cb, n_heads, *, tm):
    m, d = h.shape
    rank = g_q.shape[0]
    n_memq = w_in.shape[2] - rank
    tm = _tile(m, tm)
    row = lambda i: (i, 0)
    fixed = lambda i: (0, 0)
    stacked = lambda i: (layer, 0, 0)
    return pl.pallas_call(
        functools.partial(_mla_in_kernel, n_heads=n_heads, rank=rank),
        out_shape=(jax.ShapeDtypeStruct((m, n_heads * MLA_QK_PAD), BF16),
                   jax.ShapeDtypeStruct((m, n_memq), BF16)),
        grid=(m // tm,),
        in_specs=[
            pl.BlockSpec((tm, d), row),
            pl.BlockSpec((1, d), fixed),
            pl.BlockSpec((None,) + w_in.shape[1:], stacked),
            pl.BlockSpec((1, rank), fixed),
            pl.BlockSpec((None,) + w_uq_pad.shape[1:], stacked),
            pl.BlockSpec((tm, LANES), row),
            pl.BlockSpec((tm, LANES), row),
        ],
        out_specs=(pl.BlockSpec((tm, n_heads * MLA_QK_PAD), row),
                   pl.BlockSpec((tm, n_memq), row)),
        compiler_params=_params("parallel", fuse_inputs=[False, False, True, False, True, True, True]),
        name="mla_in",
    )(h, g.reshape(1, d), w_in, g_q.reshape(1, rank), w_uq_pad, ca, cb)


def _rope_tables(positions):
    half = MLA_ROPE_DIM // 2
    inv_freq = ROPE_THETA ** (-jnp.arange(half, dtype=F32) / half)
    ang = positions.reshape(-1).astype(F32)[:, None] * inv_freq
    cos, sin = jnp.cos(ang), jnp.sin(ang)
    zeros = jnp.zeros((ang.shape[0], LANES - MLA_ROPE_DIM), F32)
    ca = jnp.concatenate([cos, cos, zeros], axis=-1)
    cb = jnp.concatenate([-sin, sin, zeros], axis=-1)
    return ca, cb


def kernel(x, mem, positions, attn_norm_g, ffn_norm_g, a_w_in, a_w_out, b_w_in, b_q_norm_g, b_w_uq, b_w_out, mem_norm_g, w_mem_kv, kv_norm_g, w_dkv, kv_latent_g, w_ukv, ffn_w_gu, ffn_w_down, final_norm_g):
    b, s, d = x.shape
    mem_len = mem.shape[1]
    depth = attn_norm_g.shape[0]
    n_a = a_w_in.shape[0]
    n_b = b_w_in.shape[0]
    mq_w = MEM_HEADS * HEAD_DIM
    sb_heads = (a_w_in.shape[2] - mq_w) // (3 * HEAD_DIM)
    q_rank = b_q_norm_g.shape[1]
    mla_heads = b_w_uq.shape[2] // (MLA_NOPE_DIM + MLA_ROPE_DIM)

    a_w_in_b = a_w_in.astype(BF16)
    a_w_out_b = a_w_out.astype(BF16)
    b_w_in_b = b_w_in.astype(BF16)
    b_w_out_b = b_w_out.astype(BF16)
    w_ukv_b = w_ukv.astype(BF16)
    w_mem_all = jnp.transpose(w_mem_kv, (1, 0, 2)).reshape(d, depth * 2 * mq_w).astype(BF16)
    w_dkv_pad = jnp.pad(w_dkv, ((0, 0), (0, LANES - MLA_ROPE_DIM))).astype(BF16)
    w_uq_pad = jnp.pad(
        b_w_uq.reshape(n_b, q_rank, mla_heads, MLA_NOPE_DIM + MLA_ROPE_DIM),
        ((0, 0), (0, 0), (0, 0), (0, MLA_QK_PAD - MLA_NOPE_DIM - MLA_ROPE_DIM)),
    ).reshape(n_b, q_rank, mla_heads * MLA_QK_PAD).astype(BF16)
    ca, cb = _rope_tables(positions)
    a_col_scale = jnp.concatenate([jnp.full((sb_heads * HEAD_DIM,), SB_Q_PRESCALE, F32),
                                   jnp.ones((a_w_in.shape[2] - sb_heads * HEAD_DIM,), F32)])

    h = x.reshape(b * s, d)
    mkv = _norm_matmul(mem.reshape(b * mem_len, d), mem_norm_g, w_mem_all[None], 0,
                       jnp.ones((w_mem_all.shape[1],), F32), tm=NORM_MATMUL_ROWS, tn=NORM_MATMUL_COLS)
    mkv = mkv.reshape(b, mem_len, depth * 2 * mq_w)

    k_cat = v_lat = None
    for layer in range(depth):
        if layer == n_a:
            k_cat, v_lat = _latent_kv(h, kv_norm_g, w_dkv_pad, kv_latent_g, w_ukv_b, ca, cb,
                                      mla_heads, tm=LATENT_KV_ROWS)
            k_cat = k_cat.reshape(b, s, -1)
            v_lat = v_lat.reshape(b, s, -1)
        if layer < n_a:
            proj = _norm_matmul(h, attn_norm_g[layer], a_w_in_b, layer, a_col_scale,
                                tm=NORM_MATMUL_ROWS, tn=NORM_MATMUL_COLS)
            mix = _sb_attention(proj.reshape(b, s, -1), sb_heads, blk=ATTENTION_BLOCK,
                                heads_per_step=SB_HEADS_PER_STEP)
            q_src, q_block = proj, (3 * sb_heads * HEAD_DIM) // mq_w
            w_out, li = a_w_out_b, layer
        else:
            li = layer - n_a
            q_cat, mem_q = _mla_in(h, attn_norm_g[layer], b_w_in_b, b_q_norm_g[li], w_uq_pad, li,
                                   ca, cb, mla_heads, tm=MLA_IN_ROWS)
            mix = _mla_attention(q_cat.reshape(b, s, -1), k_cat, v_lat, mla_heads, blk=ATTENTION_BLOCK,
                                 heads_per_step=MLA_HEADS_PER_STEP)
            q_src, q_block = mem_q, 0
            w_out = b_w_out_b
        h, w_gu_b, w_d_b = _mem_out_proj(mix.reshape(b * s, -1), q_src, q_block, mkv, 2 * layer, w_out, li, h,
                                         ffn_w_gu, ffn_w_down, layer, tm=MEM_OUT_PROJ_ROWS)
        h = _ffn(h, ffn_norm_g[layer], w_gu_b[None], w_d_b[None], 0, final_norm_g,
                 final_norm=(layer == depth - 1), tm=FFN_ROWS, tf=FFN_HIDDEN)
    return h.reshape(b, s, d)
```
